```python
import jax, jax.numpy as jnp
from jax import lax
import numpy as np

D_MODEL = 2048
BATCH = 4
SEQ = 2048
DEPTH = 2
DEC_BATCH = 128
DEC_SEQ = 4
PAST_LEN = 16384
PAGE_SIZE = 128

N_EVEN = (DEPTH + 1) // 2
N_ODD = DEPTH // 2
MIX_WIDTH = 2 * D_MODEL
A_INNER = MIX_WIDTH // 2
A_HEAD_DIM = 64
A_HEADS = A_INNER // A_HEAD_DIM
A_GROUPS = 8
A_STATE = 128
A_CONV_W = 4
A_CONV_DIM = A_INNER + 2 * A_GROUPS * A_STATE
A_CHUNK = 128
B_HEADS = 4
B_VALUE = MIX_WIDTH // 2
B_KEY = B_VALUE // 2
B_DK = B_KEY // B_HEADS
B_DV = B_VALUE // B_HEADS
B_GATE_RANK = 16
B_GATE_TAU = 16.0
B_CHUNK = 64
IN0_DIM = A_INNER + A_CONV_DIM + A_HEADS + 2 * B_KEY + 2 * B_VALUE + B_GATE_RANK
C_WIDTH = MIX_WIDTH
C_GROUPS = 8
C_GROUP_DIM = C_WIDTH // C_GROUPS
C_CHUNK = 128
D_FF = 64 * ((-(-8 * D_MODEL // 3)) // 64 + (1 if (-(-8 * D_MODEL // 3)) % 64 else 0))
N_EXPERTS = 8
TOP_K = 2
EPS = 1e-6

kernel_name = 'hybrid_ssd_gla_chunkmlp_moe_step'


def rmsnorm(x, g):
    xf = x.astype(jnp.float32)
    y = xf * lax.rsqrt(jnp.mean(xf * xf, axis=-1, keepdims=True) + EPS)
    return (y * g.astype(jnp.float32)).astype(x.dtype)


def adaln(c, w, b):
    mod = jax.nn.silu(c) @ w + b
    return jnp.split(mod, 6, axis=-1)


def modulate(h, shift, scale):
    return h * (1.0 + scale[:, None]) + shift[:, None]


def split_sizes(a, sizes):
    idx = [int(i) for i in np.cumsum(sizes)[:-1]]
    return jnp.split(a, idx, axis=-1)


def to_chunks(t, cl, nc):
    pad = nc * cl - t.shape[1]
    t = jnp.pad(t, [(0, 0), (0, pad)] + [(0, 0)] * (t.ndim - 2))
    return t.reshape(t.shape[0], nc, cl, *t.shape[2:])


def swiglu(h, w1, w3, w2):
    return (jax.nn.silu(h @ w1) * (h @ w3)) @ w2


def causal_conv(xbc, buf, w, b):
    full = jnp.concatenate([buf.astype(xbc.dtype), xbc], axis=1)
    y = lax.conv_general_dilated(full, w[:, None, :].astype(full.dtype), (1,), 'VALID',
                                 dimension_numbers=('NWC', 'WIO', 'NWC'),
                                 feature_group_count=A_CONV_DIM) + b
    new_buf = full[:, full.shape[1] - (A_CONV_W - 1):]
    return jax.nn.silu(y), new_buf


def ssd_scan(x, dt, a, bm, cm, h0):
    bsz, L = x.shape[:2]
    g, r = A_GROUPS, A_HEADS // A_GROUPS
    cl = min(A_CHUNK, L)
    nc = -(-L // cl)
    xdt = to_chunks((x * dt[..., None]).reshape(bsz, L, g, r, A_HEAD_DIM), cl, nc)
    da = to_chunks((dt * a).reshape(bsz, L, g, r), cl, nc)
    bc = to_chunks(bm, cl, nc)
    cc = to_chunks(cm, cl, nc)
    acum = jnp.cumsum(da, axis=2)
    causal = jnp.tril(jnp.ones((cl, cl), bool))
    seg = acum[:, :, :, None] - acum[:, :, None, :]
    decay = jnp.exp(jnp.where(causal[:, :, None, None], seg, -jnp.inf))
    cb = jnp.einsum('bctgn,bcsgn->bctsg', cc, bc)
    y_diag = jnp.einsum('bctsgr,bcsgrp->bctgrp', cb[..., None] * decay, xdt)
    to_end = jnp.exp(acum[:, :, -1:] - acum)
    states = jnp.einsum('bcsgn,bcsgrp->bcgrpn', bc, xdt * to_end[..., None])
    chunk_decay = jnp.exp(acum[:, :, -1])

    def step(hs, inp):
        st, dec = inp
        return hs * dec[..., None, None] + st, hs

    h_last, h_prev = lax.scan(step, h0.reshape(bsz, g, r, A_HEAD_DIM, A_STATE),
                              (jnp.moveaxis(states, 1, 0), jnp.moveaxis(chunk_decay, 1, 0)))
    y_off = jnp.einsum('bctgn,cbgrpn->bctgrp', cc, h_prev) * jnp.exp(acum)[..., None]
    y = (y_diag + y_off).reshape(bsz, nc * cl, A_HEADS, A_HEAD_DIM)[:, :L]
    return y, h_last.reshape(bsz, A_HEADS, A_HEAD_DIM, A_STATE)


def gla_chunked(q, k, v, log_a, s0):
    bsz, L = q.shape[:2]
    cl = min(B_CHUNK, L)
    nc = -(-L // cl)
    qc, kc, vc, gc = (to_chunks(t, cl, nc) for t in (q, k, v, log_a))
    bcum = jnp.cumsum(gc, axis=2)
    q_t = qc * jnp.exp(bcum)
    k_t = kc * jnp.exp(-bcum)
    causal = jnp.tril(jnp.ones((cl, cl), bool))
    att = jnp.where(causal, jnp.einsum('bcthk,bcshk->bchts', q_t, k_t), 0.0)
    o_intra = jnp.einsum('bchts,bcshv->bcthv', att, vc)
    b_last = bcum[:, :, -1]
    ds = jnp.einsum('bcshk,bcshv->bchkv', kc * jnp.exp(b_last[:, :, None] - bcum), vc)

    def step(s, inp):
        d_s, bl = inp
        return s * jnp.exp(bl)[..., None] + d_s, s

    s_last, s_prev = lax.scan(step, s0, (jnp.moveaxis(ds, 1, 0), jnp.moveaxis(b_last, 1, 0)))
    o_inter = jnp.einsum('bcthk,cbhkv->bcthv', q_t, s_prev)
    o = (o_intra + o_inter).reshape(bsz, nc * cl, B_HEADS, B_DV)[:, :L]
    return o, s_last


def mixer_ab(h, ssm0, conv0, gla0, w_in, conv_w, conv_b, dt_bias, a_log, d_skip, a_norm,
             gla_wa2, gla_ba, gla_norm, w_out):
    f32 = jnp.float32
    bsz, L, _ = h.shape
    z, xbc, dt_raw, q, k, v, r, g_lr = split_sizes(
        h @ w_in, [A_INNER, A_CONV_DIM, A_HEADS, B_KEY, B_KEY, B_VALUE, B_VALUE, B_GATE_RANK])
    xbc_c, conv_new = causal_conv(xbc, conv0, conv_w, conv_b)
    xs, bm, cm = split_sizes(xbc_c, [A_INNER, A_GROUPS * A_STATE, A_GROUPS * A_STATE])
    xs = xs.reshape(bsz, L, A_HEADS, A_HEAD_DIM).astype(f32)
    dt = jax.nn.softplus(dt_raw.astype(f32) + dt_bias.astype(f32))
    a = -jnp.exp(a_log.astype(f32))
    y_a, ssm_new = ssd_scan(xs, dt, a,
                            bm.reshape(bsz, L, A_GROUPS, A_STATE).astype(f32),
                            cm.reshape(bsz, L, A_GROUPS, A_STATE).astype(f32),
                            ssm0.astype(f32))
    y_a = (y_a + d_skip.astype(f32)[:, None] * xs).reshape(bsz, L, A_INNER)
    y_a = rmsnorm(y_a * jax.nn.silu(z.astype(f32)), a_norm)
    log_a = jax.nn.log_sigmoid((g_lr @ gla_wa2 + gla_ba).astype(f32)) / B_GATE_TAU
    qh = q.reshape(bsz, L, B_HEADS, B_DK).astype(f32) * (B_DK ** -0.5)
    kh = k.reshape(bsz, L, B_HEADS, B_DK).astype(f32)
    vh = v.reshape(bsz, L, B_HEADS, B_DV).astype(f32)
    o, gla_new = gla_chunked(qh, kh, vh, log_a.reshape(bsz, L, B_HEADS, B_DK), gla0.astype(f32))
    o = rmsnorm(o, gla_norm.reshape(B_HEADS, B_DV))
    y_b = o.reshape(bsz, L, B_VALUE) * jax.nn.silu(r.astype(f32))
    y = jnp.concatenate([y_a, y_b], axis=-1).astype(h.dtype) @ w_out
    return y, ssm_new.astype(ssm0.dtype), conv_new.astype(conv0.dtype), gla_new.astype(gla0.dtype)


def chunk_mix(v, ws, bs):
    bsz, L = v.shape[:2]
    cl = min(C_CHUNK, L)
    nc = -(-L // cl)
    vp = to_chunks(v, cl, nc)
    w = jnp.tril(ws[:, :cl, :cl]).astype(v.dtype)
    y = jnp.einsum('gts,bnsge->bntge', w, vp) + bs[:, :cl].T.astype(v.dtype)[None, None, :, :, None]
    return y.reshape(bsz, nc * cl, C_GROUPS, C_GROUP_DIM)[:, :L]


def mixer_c(h, w_in, ln_g, ln_b, ws, bs, w_out):
    f32 = jnp.float32
    bsz, L, _ = h.shape
    u, v = jnp.split(jax.nn.gelu(h @ w_in), 2, axis=-1)
    vf = v.reshape(bsz, L, C_GROUPS, C_GROUP_DIM).astype(f32)
    mu = jnp.mean(vf, axis=-1, keepdims=True)
    var = jnp.mean(jnp.square(vf - mu), axis=-1, keepdims=True)
    vn = (vf - mu) * lax.rsqrt(var + EPS) * ln_g.reshape(C_GROUPS, C_GROUP_DIM).astype(f32) \
        + ln_b.reshape(C_GROUPS, C_GROUP_DIM).astype(f32)
    mixed = chunk_mix(vn, ws.astype(f32), bs.astype(f32))
    y = (u.reshape(bsz, L, C_GROUPS, C_GROUP_DIM).astype(f32) * mixed).reshape(bsz, L, C_WIDTH)
    return y.astype(h.dtype) @ w_out, vn.astype(h.dtype)


def moe(h, router_w, w1, w3, w2):
    bsz, L, d = h.shape
    t = h.reshape(-1, d)
    logits = (t @ router_w).astype(jnp.float32)
    top_v, top_i = lax.top_k(logits, TOP_K)
    gates = jax.nn.softmax(top_v, axis=-1)
    dense_gate = jnp.sum(jax.nn.one_hot(top_i, N_EXPERTS, dtype=jnp.float32) * gates[..., None], axis=1)
    y = jnp.zeros(t.shape, jnp.float32)
    for e in range(N_EXPERTS):
        y = y + dense_gate[:, e:e + 1] * swiglu(t, w1[e], w3[e], w2[e]).astype(jnp.float32)
    return y.astype(h.dtype).reshape(bsz, L, d)


def even_layer(x, c, ssm0, conv0, gla0, P, i):
    sm, scm, gm, sf, scf, gf = adaln(c, P['ada_w0'][i], P['ada_b0'][i])
    h = modulate(rmsnorm(x, P['norm_mix0'][i]), sm, scm)
    y, ssm_new, conv_new, gla_new = mixer_ab(
        h, ssm0, conv0, gla0, P['w_in0'][i], P['conv_w'][i], P['conv_b'][i], P['dt_bias'][i],
        P['a_log'][i], P['d_skip'][i], P['a_norm'][i], P['gla_wa2'][i], P['gla_ba'][i],
        P['gla_norm'][i], P['w_out0'][i])
    x = x + gm[:, None] * y
    h = modulate(rmsnorm(x, P['norm_ffn0'][i]), sf, scf)
    x = x + gf[:, None] * swiglu(h, P['ffn_w1'][i], P['ffn_w3'][i], P['ffn_w2'][i])
    return x, ssm_new, conv_new, gla_new


def odd_layer(x, c, P, i):
    sm, scm, gm, sf, scf, gf = adaln(c, P['ada_w1'][i], P['ada_b1'][i])
    h = modulate(rmsnorm(x, P['norm_mix1'][i]), sm, scm)
    y, v_rows = mixer_c(h, P['c_w_in'][i], P['c_ln_g'][i], P['c_ln_b'][i], P['c_ws'][i],
                        P['c_bs'][i], P['c_w_out'][i])
    x = x + gm[:, None] * y
    h = modulate(rmsnorm(x, P['norm_ffn1'][i]), sf, scf)
    x = x + gf[:, None] * moe(h, P['router_w'][i], P['moe_w1'][i], P['moe_w3'][i], P['moe_w2'][i])
    return x, v_rows


def run_trunk(x, c, ssm_in, conv_in, gla_in, P):
    ssm_out, conv_out, gla_out, v_out = [], [], [], []
    for layer in range(DEPTH):
        i = layer // 2
        if layer % 2 == 0:
            x, s, cv, g = even_layer(x, c, ssm_in[i], conv_in[i], gla_in[i], P, i)
            ssm_out.append(s)
            conv_out.append(cv)
            gla_out.append(g)
        else:
            x, vr = odd_layer(x, c, P, i)
            v_out.append(vr)
    return rmsnorm(x, P['norm_f']), ssm_out, conv_out, gla_out, v_out


def setup_inputs(seed: int = 0) -> dict:
    key = jax.random.key(seed)
    ks = iter(jax.random.split(key, 48))
    f32 = jnp.float32

    def nrm(shape, scale):
        return jax.random.normal(next(ks), shape, f32) * scale

    def gain(shape):
        return 1.0 + nrm(shape, 0.02)

    D = D_MODEL
    dt0 = jnp.exp(jax.random.uniform(next(ks), (N_EVEN, A_HEADS), f32, np.log(1e-3), np.log(1e-1)))
    inp = {
        'x_prompt': nrm((BATCH, SEQ, D), 1.0),
        'x_sample': nrm((DEC_BATCH, DEC_SEQ, D), 1.0),
        'state_ssm': nrm((N_EVEN, DEC_BATCH, A_HEADS, A_HEAD_DIM, A_STATE), 0.1),
        'state_conv': nrm((N_EVEN, DEC_BATCH, A_CONV_W - 1, A_CONV_DIM), 1.0),
        'state_gla': nrm((N_EVEN, DEC_BATCH, B_HEADS, B_DK, B_DV), 0.5),
        'c_prompt': nrm((BATCH, D), 1.0),
        'c_sample': nrm((DEC_BATCH, D), 1.0),
        'ada_w0': nrm((N_EVEN, D, 6 * D), 0.5 * D ** -0.5),
        'ada_b0': nrm((N_EVEN, 6 * D), 0.02),
        'norm_mix0': gain((N_EVEN, D)),
        'norm_ffn0': gain((N_EVEN, D)),
        'w_in0': nrm((N_EVEN, D, IN0_DIM), D ** -0.5),
        'conv_w': nrm((N_EVEN, A_CONV_W, A_CONV_DIM), 0.5),
        'conv_b': nrm((N_EVEN, A_CONV_DIM), 0.02),
        'dt_bias': dt0 + jnp.log(-jnp.expm1(-dt0)),
        'a_log': jnp.log(jax.random.uniform(next(ks), (N_EVEN, A_HEADS), f32, 1.0, 16.0)),
        'd_skip': gain((N_EVEN, A_HEADS)),
        'a_norm': gain((N_EVEN, A_INNER)),
        'gla_wa2': nrm((N_EVEN, B_GATE_RANK, B_KEY), B_GATE_RANK ** -0.5),
        'gla_ba': nrm((N_EVEN, B_KEY), 0.1),
        'gla_norm': gain((N_EVEN, B_VALUE)),
        'w_out0': nrm((N_EVEN, MIX_WIDTH, D), MIX_WIDTH ** -0.5),
        'ffn_w1': nrm((N_EVEN, D, D_FF), D ** -0.5),
        'ffn_w3': nrm((N_EVEN, D, D_FF), D ** -0.5),
        'ffn_w2': nrm((N_EVEN, D_FF, D), D_FF ** -0.5),
        'ada_w1': nrm((N_ODD, D, 6 * D), 0.5 * D ** -0.5),
        'ada_b1': nrm((N_ODD, 6 * D), 0.02),
        'norm_mix1': gain((N_ODD, D)),
        'norm_ffn1': gain((N_ODD, D)),
        'c_w_in': nrm((N_ODD, D, 2 * C_WIDTH), D ** -0.5),
        'c_ln_g': gain((N_ODD, C_WIDTH)),
        'c_ln_b': nrm((N_ODD, C_WIDTH), 0.02),
        'c_ws': nrm((N_ODD, C_GROUPS, C_CHUNK, C_CHUNK), 0.5 * C_CHUNK ** -0.5),
        'c_bs': 1.0 + nrm((N_ODD, C_GROUPS, C_CHUNK), 0.1),
        'c_w_out': nrm((N_ODD, C_WIDTH, D), C_WIDTH ** -0.5),
        'router_w': nrm((N_ODD, D, N_EXPERTS), D ** -0.5),
        'moe_w1': nrm((N_ODD, N_EXPERTS, D, D_FF), D ** -0.5),
        'moe_w3': nrm((N_ODD, N_EXPERTS, D, D_FF), D ** -0.5),
        'moe_w2': nrm((N_ODD, N_EXPERTS, D_FF, D), D_FF ** -0.5),
        'norm_f': gain((D,)),
    }
    return inp


def reference(x_prompt, x_sample, state_ssm, state_conv, state_gla, c_prompt, c_sample,
              ada_w0, ada_b0, norm_mix0, norm_ffn0, w_in0, conv_w, conv_b, dt_bias, a_log, d_skip,
              a_norm, gla_wa2, gla_ba, gla_norm, w_out0, ffn_w1, ffn_w3, ffn_w2,
              ada_w1, ada_b1, norm_mix1, norm_ffn1, c_w_in, c_ln_g, c_ln_b, c_ws, c_bs, c_w_out,
              router_w, moe_w1, moe_w3, moe_w2, norm_f):
    P = dict(ada_w0=ada_w0, ada_b0=ada_b0, norm_mix0=norm_mix0, norm_ffn0=norm_ffn0, w_in0=w_in0,
             conv_w=conv_w, conv_b=conv_b, dt_bias=dt_bias, a_log=a_log, d_skip=d_skip, a_norm=a_norm,
             gla_wa2=gla_wa2, gla_ba=gla_ba, gla_norm=gla_norm, w_out0=w_out0, ffn_w1=ffn_w1,
             ffn_w3=ffn_w3, ffn_w2=ffn_w2, ada_w1=ada_w1, ada_b1=ada_b1, norm_mix1=norm_mix1,
             norm_ffn1=norm_ffn1, c_w_in=c_w_in, c_ln_g=c_ln_g, c_ln_b=c_ln_b, c_ws=c_ws, c_bs=c_bs,
             c_w_out=c_w_out, router_w=router_w, moe_w1=moe_w1, moe_w3=moe_w3, moe_w2=moe_w2,
             norm_f=norm_f)
    bp = x_prompt.shape[0]
    ssm_zero = jnp.zeros((N_EVEN, bp, A_HEADS, A_HEAD_DIM, A_STATE), x_prompt.dtype)
    conv_zero = jnp.zeros((N_EVEN, bp, A_CONV_W - 1, A_CONV_DIM), x_prompt.dtype)
    gla_zero = jnp.zeros((N_EVEN, bp, B_HEADS, B_DK, B_DV), x_prompt.dtype)
    y_prompt, ssm_p, conv_p, gla_p, _ = run_trunk(x_prompt, c_prompt, ssm_zero, conv_zero, gla_zero, P)
    y_sample, ssm_s, conv_s, gla_s, v_s = run_trunk(x_sample, c_sample, state_ssm, state_conv, state_gla, P)
    ssm_prompt = jnp.stack(ssm_p)
    conv_prompt = jnp.stack(conv_p)
    gla_prompt = jnp.stack(gla_p)
    ssm_sample = jnp.stack(ssm_s)
    conv_sample = jnp.stack(conv_s)
    gla_sample = jnp.stack(gla_s)
    cmlp_v_sample = jnp.stack(v_s)
    return (y_prompt, y_sample, ssm_prompt, conv_prompt, gla_prompt, ssm_sample, conv_sample, gla_sample, cmlp_v_sample)
```

```python
import functools
import math

import jax
import jax.numpy as jnp
from jax import lax
from jax.experimental import pallas as pl
from jax.experimental.pallas import tpu as pltpu

F32 = jnp.float32
BF16 = jnp.bfloat16
I32 = jnp.int32

EPS = 1e-6
LANES = 128
SUBLANES = 8
SSD_CHUNK = 128
GLA_CHUNK = 64
GLA_GATE_TAU = 16.0
TOP_K = 2
V7X_VMEM_BYTES = 64 * 1024 * 1024
VMEM_LIMIT = V7X_VMEM_BYTES - 8 * 1024 * 1024

NT_DIMS = (((1,), (1,)), ((), ()))
TN_DIMS = (((0,), (0,)), ((), ()))


def _sigmoid(x):
    return 1.0 / (1.0 + jnp.exp(-x))


def _silu(x):
    return x * _sigmoid(x)


def _softplus(x):
    return jnp.maximum(x, 0.0) + jnp.log1p(jnp.exp(-jnp.abs(x)))


def _gelu_tanh(x):
    c = math.sqrt(2.0 / math.pi)
    return x * (0.5 * (1.0 + jnp.tanh(c * (x + 0.044715 * (x * x * x)))))


def _dot(a, b, dims=None):
    if dims is None:
        return jnp.dot(a, b, preferred_element_type=F32)
    return lax.dot_general(a, b, dims, preferred_element_type=F32)


def _split3(x):
    hi = x.astype(BF16)
    r1 = x - hi.astype(F32)
    mid = r1.astype(BF16)
    lo = (r1 - mid.astype(F32)).astype(BF16)
    return hi, mid, lo


def _dot_f32_lhs(a_f32, b_exact, dims=None):
    p0, p1, p2 = _split3(a_f32)
    return _dot(p0, b_exact, dims) + _dot(p1, b_exact, dims) + _dot(p2, b_exact, dims)


def _dot_f32_rhs(a_exact, b_f32, dims=None):
    p0, p1, p2 = _split3(b_f32)
    return _dot(a_exact, p0, dims) + _dot(a_exact, p1, dims) + _dot(a_exact, p2, dims)


def _iota(shape, axis):
    return lax.broadcasted_iota(I32, shape, axis)


def _params(*sem):
    return pltpu.CompilerParams(dimension_semantics=sem, vmem_limit_bytes=VMEM_LIMIT)


def _full(shape):
    nd = len(shape)
    return pl.BlockSpec(shape, lambda *_: (0,) * nd)


def _mm_kernel(*refs, n_lhs, k_sizes, n_ptiles, dual, lhs_act, has_bias, act):
    pos = 0
    lhs = []
    for _ in range(n_lhs):
        if dual:
            lhs.append((refs[pos], refs[pos + 1]))
            pos += 2
        else:
            lhs.append((refs[pos], None))
            pos += 1
    w_ref = refs[pos]
    pos += 1
    b_ref = None
    if has_bias:
        b_ref = refs[pos]
        pos += 1
    o_ref, wb_ref = refs[pos], refs[pos + 1]
    i = pl.program_id(1)

    @pl.when(i == 0)
    def _cast():
        wb_ref[...] = w_ref[...].astype(BF16)

    acc = None
    k0 = 0
    for (xp_ref, xs_ref), kk in zip(lhs, k_sizes):
        x = xp_ref[...]
        if xs_ref is not None:
            x = jnp.where(i < n_ptiles, x, xs_ref[...])
        if lhs_act == "silu":
            x = _silu(x.astype(F32))
        part = _dot(x.astype(BF16), wb_ref[k0:k0 + kk, :])
        acc = part if acc is None else acc + part
        k0 += kk
    if has_bias:
        acc = acc + b_ref[...]
    if act == "gelu":
        acc = _gelu_tanh(acc)
    o_ref[...] = acc.astype(o_ref.dtype)


def _matmul(lhs, w, *, tm, tn, n_cols=None, col_block0=0, bias=None, lhs_act=None, act=None,
            out_dtype=F32):
    dual = isinstance(lhs[0], tuple)
    k_sizes = tuple((p[0] if dual else p).shape[1] for p in lhs)
    k_total = sum(k_sizes)
    assert w.shape[0] == k_total
    if dual:
        rows_p, rows_s = lhs[0][0].shape[0], lhs[0][1].shape[0]
        assert rows_p % tm == 0 and rows_s == tm
        n_ptiles = rows_p // tm
        n_rows = rows_p + rows_s
    else:
        n_rows = lhs[0].shape[0]
        assert n_rows % tm == 0
        n_ptiles = n_rows // tm
    n_cols = w.shape[1] if n_cols is None else n_cols
    assert n_cols % tn == 0
    grid = (n_cols // tn, n_rows // tm)

    args, in_specs = [], []
    for part, kk in zip(lhs, k_sizes):
        if dual:
            args += [part[0], part[1]]
            in_specs += [pl.BlockSpec((tm, kk), lambda j, i: (jnp.minimum(i, n_ptiles - 1), 0)),
                         pl.BlockSpec((tm, kk), lambda j, i: (0, 0))]
        else:
            args.append(part)
            in_specs.append(pl.BlockSpec((tm, kk), lambda j, i: (i, 0)))
    args.append(w)
    in_specs.append(pl.BlockSpec((k_total, tn), lambda j, i: (0, j + col_block0)))
    if bias is not None:
        args.append(bias.reshape(1, -1))
        in_specs.append(pl.BlockSpec((1, tn), lambda j, i: (0, j + col_block0)))
    kern = functools.partial(_mm_kernel, n_lhs=len(lhs), k_sizes=k_sizes, n_ptiles=n_ptiles, dual=dual,
                             lhs_act=lhs_act, has_bias=bias is not None, act=act)
    return pl.pallas_call(
        kern,
        grid=grid,
        in_specs=in_specs,
        out_specs=pl.BlockSpec((tm, tn), lambda j, i: (i, j)),
        out_shape=jax.ShapeDtypeStruct((n_rows, n_cols), out_dtype),
        scratch_shapes=[pltpu.VMEM((k_total, tn), BF16)],
        compiler_params=_params("arbitrary", "arbitrary"),
    )(*args)


def _norm_kernel(*refs, n_ptiles, has_resid, has_mod):
    i = pl.program_id(0)
    is_prompt = i < n_ptiles
    refs = list(refs)
    x = refs.pop(0)[...]
    if has_resid:
        y_ref, gp_ref, gs_ref = refs.pop(0), refs.pop(0), refs.pop(0)
        x = x + jnp.where(is_prompt, gp_ref[...], gs_ref[...]) * y_ref[...]
    gain_ref = refs.pop(0)
    if has_mod:
        shp_ref, shs_ref, scp_ref, scs_ref = refs.pop(0), refs.pop(0), refs.pop(0), refs.pop(0)
    if has_resid and has_mod:
        xo_ref = refs.pop(0)
        xo_ref[...] = x
    h_ref = refs.pop(0)
    ms = jnp.mean(x * x, axis=-1, keepdims=True)
    h = (x * lax.rsqrt(ms + EPS)) * gain_ref[...]
    if has_mod:
        scale = jnp.where(is_prompt, scp_ref[...], scs_ref[...])
        shift = jnp.where(is_prompt, shp_ref[...], shs_ref[...])
        h = h * (1.0 + scale) + shift
    h_ref[...] = h.astype(h_ref.dtype)


def _norm(x, gain, *, rows_p, len_p, tm, resid=None, mod=None, h_dtype=BF16):
    n_rows, d = x.shape
    assert rows_p % tm == 0 and (n_rows - rows_p) % tm == 0 and len_p % tm == 0
    n_ptiles = rows_p // tm
    tiles_per_batch = len_p // tm
    n_bp = rows_p // len_p

    row_spec = pl.BlockSpec((tm, d), lambda i: (i, 0))
    p_spec = pl.BlockSpec((None, 1, d), lambda i: (jnp.minimum(i // tiles_per_batch, n_bp - 1), 0, 0))
    s_spec = pl.BlockSpec((tm, d), lambda i: (jnp.maximum(i - n_ptiles, 0), 0))

    args, in_specs = [x], [row_spec]
    if resid is not None:
        args += list(resid)
        in_specs += [row_spec, p_spec, s_spec]
    args.append(gain.reshape(1, d))
    in_specs.append(_full((1, d)))
    if mod is not None:
        args += list(mod)
        in_specs += [p_spec, s_spec, p_spec, s_spec]
    both = resid is not None and mod is not None
    out_shape = [jax.ShapeDtypeStruct((n_rows, d), h_dtype)]
    out_specs = [row_spec]
    if both:
        out_shape.insert(0, jax.ShapeDtypeStruct((n_rows, d), F32))
        out_specs.insert(0, row_spec)
    kern = functools.partial(_norm_kernel, n_ptiles=n_ptiles, has_resid=resid is not None,
                             has_mod=mod is not None)
    out = pl.pallas_call(
        kern,
        grid=(n_rows // tm,),
        in_specs=in_specs,
        out_specs=out_specs,
        out_shape=out_shape,
        compiler_params=_params("arbitrary"),
    )(*args)
    return tuple(out) if both else out[0]


def _pad_rows(val, pad_ref, n_rows):
    if pad_ref is None:
        return val
    pad_ref[0:n_rows, :] = val
    return pad_ref[...]


def _ssd_kernel(z_ref, x_ref, bc_ref, dt_ref, cinit_ref, cw_ref, cb_ref, dtb_ref, alog_ref, dexp_ref,
                anorm_ref, e64_ref, e64t_ref, e128_ref, h0_ref, y_ref, hout_ref, h_ref, xpad_ref,
                *pad_refs, t_rows, n_heads, head_dim, n_groups, n_state, conv_w):
    c = pl.program_id(1)
    tp = SSD_CHUNK
    hp = n_heads * head_dim
    gn = n_groups * n_state
    heads_per_group = n_heads // n_groups
    group_w = heads_per_group * head_dim
    padded = t_rows < tp
    if padded:
        da_pad, xdt_pad, b_pad, xw_pad = pad_refs

    @pl.when(c == 0)
    def _init():
        h_ref[...] = h0_ref[...]
        xpad_ref[0:SUBLANES, :] = cinit_ref[...]
        for r in pad_refs:
            r[...] = jnp.zeros(r.shape, r.dtype)

    xpad_ref[SUBLANES:SUBLANES + t_rows, 0:hp] = x_ref[...]
    xpad_ref[SUBLANES:SUBLANES + t_rows, hp:] = bc_ref[...]
    acc = cb_ref[...] + cw_ref[0:1, :] * xpad_ref[SUBLANES - conv_w + 1:SUBLANES - conv_w + 1 + t_rows, :]
    for k in range(1, conv_w):
        off = SUBLANES - conv_w + 1 + k
        acc = acc + cw_ref[k:k + 1, :] * xpad_ref[off:off + t_rows, :]
    xc = _silu(acc)
    if not padded:
        xpad_ref[0:SUBLANES, :] = xpad_ref[t_rows:t_rows + SUBLANES, :]
    xs = xc[:, :hp]
    bm = xc[:, hp:hp + gn]
    cm = xc[:, hp + gn:]

    lane_row = _iota((1, LANES), 1)
    dt = _softplus(dt_ref[...] + dtb_ref[...])
    a = jnp.where(lane_row < n_heads, -jnp.exp(alog_ref[...]), 0.0)
    da_p = _pad_rows(dt * a, da_pad if padded else None, t_rows)
    tril = (_iota((tp, tp), 1) <= _iota((tp, tp), 0)).astype(BF16)
    acum_p = _dot_f32_rhs(tril, da_p)
    acum = acum_p[0:t_rows, :]
    acum_t = acum_p.T
    dt_exp = _dot_f32_lhs(dt, e64_ref[...])
    ac_exp = _dot_f32_lhs(acum, e64_ref[...])
    al_exp = ac_exp[t_rows - 1:t_rows, :]
    col = _dot_f32_lhs(acum, e128_ref[...])

    xdt = xs * dt_exp
    xdt_p = _pad_rows(xdt, xdt_pad if padded else None, t_rows).astype(BF16)
    b_p = _pad_rows(bm, b_pad if padded else None, t_rows).astype(BF16)
    c_b = cm.astype(BF16)
    xw = xdt * jnp.exp(al_exp - ac_exp)
    xw_p = _pad_rows(xw, xw_pad if padded else None, t_rows).astype(BF16)
    h_b = h_ref[...].astype(BF16)

    causal = _iota((t_rows, tp), 1) <= _iota((t_rows, tp), 0)
    lane_lo = _iota((tp, LANES), 1) < head_dim
    y_diag, y_off, new_states = [], [], []
    for g in range(n_groups):
        c_g = c_b[:, g * n_state:(g + 1) * n_state]
        b_g = b_p[:, g * n_state:(g + 1) * n_state]
        cb = _dot(c_g, b_g, NT_DIMS)
        for pj in range(heads_per_group // 2):
            h1 = g * heads_per_group + 2 * pj
            ms = []
            for hh in (h1, h1 + 1):
                seg = col[:, hh * LANES:(hh + 1) * LANES] - acum_t[hh:hh + 1, :]
                ms.append(jnp.where(causal, jnp.exp(seg), 0.0) * cb)
            lhs = jnp.concatenate(ms, axis=1).astype(BF16)
            xp = xdt_p[:, h1 * head_dim:h1 * head_dim + LANES]
            rhs = jnp.concatenate([jnp.where(lane_lo, xp, jnp.zeros_like(xp)),
                                   jnp.where(lane_lo, jnp.zeros_like(xp), xp)], axis=0)
            y_diag.append(_dot(lhs, rhs))
        h_g = h_b[g * group_w:(g + 1) * group_w, :]
        y_off.append(_dot(c_g, h_g, NT_DIMS))
        new_states.append(_dot(xw_p[:, g * group_w:(g + 1) * group_w], b_g, TN_DIMS))

    y = (jnp.concatenate(y_diag, axis=1) + jnp.concatenate(y_off, axis=1) * jnp.exp(ac_exp)
         + dexp_ref[...] * xs)

    sel = (_iota((tp, LANES), 0) == tp - 1).astype(BF16)
    a_last_t = _dot_f32_lhs(acum_t, sel)
    chunk_decay = jnp.exp(_dot_f32_rhs(e64t_ref[...], a_last_t))
    h_ref[...] = h_ref[...] * chunk_decay + jnp.concatenate(new_states, axis=0)

    yz = y * _silu(z_ref[...])
    ms = jnp.mean(yz * yz, axis=-1, keepdims=True)
    y_ref[...] = ((yz * lax.rsqrt(ms + EPS)) * anorm_ref[...]).astype(y_ref.dtype)

    @pl.when(c == pl.num_programs(1) - 1)
    def _fin():
        hout_ref[...] = h_ref[...]


def _ssd(z_x_bc, dt_g, conv_init, h0, consts, *, n_seq, n_chunks, t_rows, dims, rows3d, out_dtype):
    n_heads, head_dim, n_groups, n_state, conv_w = dims
    hp, gn = n_heads * head_dim, n_groups * n_state
    assert 2 * head_dim == LANES and n_state == LANES and n_heads <= LANES
    assert (n_heads // n_groups) % 2 == 0 and hp % (2 * gn) == 0 and hp % LANES == 0
    bc_blk = (2 * hp) // (2 * gn)
    if rows3d:
        def spec(width, blk):
            return pl.BlockSpec((None, t_rows, width), lambda b, c: (b, 0, blk))
        y_shape = (n_seq, t_rows, hp)
    else:
        def spec(width, blk):
            return pl.BlockSpec((t_rows, width), lambda b, c: (b * n_chunks + c, blk))
        y_shape = (n_seq * n_chunks * t_rows, hp)
    cw, cb, dtb, alog, dexp, anorm, e64, e64t, e128 = consts
    cdim = hp + 2 * gn
    in_specs = [spec(hp, 0), spec(hp, 1), spec(2 * gn, bc_blk), spec(LANES, 0),
                pl.BlockSpec((None, SUBLANES, cdim), lambda b, c: (b, 0, 0)),
                _full(cw.shape), _full(cb.shape), _full(dtb.shape), _full(alog.shape), _full(dexp.shape),
                _full(anorm.shape), _full(e64.shape), _full(e64t.shape), _full(e128.shape),
                pl.BlockSpec((None, hp, n_state), lambda b, c: (b, 0, 0))]
    scratch = [pltpu.VMEM((hp, n_state), F32), pltpu.VMEM((SUBLANES + max(t_rows, SUBLANES), cdim), F32)]
    if t_rows < SSD_CHUNK:
        scratch += [pltpu.VMEM((SSD_CHUNK, LANES), F32), pltpu.VMEM((SSD_CHUNK, hp), F32),
                    pltpu.VMEM((SSD_CHUNK, gn), F32), pltpu.VMEM((SSD_CHUNK, hp), F32)]
    kern = functools.partial(_ssd_kernel, t_rows=t_rows, n_heads=n_heads, head_dim=head_dim,
                             n_groups=n_groups, n_state=n_state, conv_w=conv_w)
    return pl.pallas_call(
        kern,
        grid=(n_seq, n_chunks),
        in_specs=in_specs,
        out_specs=[pl.BlockSpec(*((None, t_rows, hp), lambda b, c: (b, 0, 0)) if rows3d else
                                ((t_rows, hp), lambda b, c: (b * n_chunks + c, 0))),
                   pl.BlockSpec((None, hp, n_state), lambda b, c: (b, 0, 0))],
        out_shape=[jax.ShapeDtypeStruct(y_shape, out_dtype),
                   jax.ShapeDtypeStruct((n_seq, hp, n_state), F32)],
        scratch_shapes=scratch,
        compiler_params=_params("arbitrary", "arbitrary"),
    )(z_x_bc, z_x_bc, z_x_bc, dt_g, conv_init, cw, cb, dtb, alog, dexp, anorm, e64, e64t, e128, h0)


def _gla_kernel(q_ref, k_ref, v_ref, r_ref, g_ref, wa_ref, ba_ref, gnorm_ref, s0_ref, y_ref, sout_ref,
                s_ref, *pad_refs, t_rows, t_chunk, n_heads, dk, dv):
    c = pl.program_id(1)
    tp = GLA_CHUNK
    padded = t_chunk < tp
    if padded:
        la_pad, kt_pad, kw_pad, v_pad = pad_refs

    @pl.when(c == 0)
    def _init():
        s_ref[...] = s0_ref[...]
        for r in pad_refs:
            r[...] = jnp.zeros(r.shape, r.dtype)

    wa_b = wa_ref[...].astype(BF16)
    tril = (_iota((t_chunk, tp), 1) <= _iota((t_chunk, tp), 0))
    tril_b = tril.astype(BF16)
    ones_b = jnp.ones((tp, LANES), BF16)
    scale = dk ** -0.5
    for sub in range(t_rows // t_chunk):
        rows = slice(sub * t_chunk, (sub + 1) * t_chunk)
        la_raw = _dot(g_ref[rows, :].astype(BF16), wa_b) + ba_ref[...]
        la = -_softplus(-la_raw) * (1.0 / GLA_GATE_TAU)
        la_p = _pad_rows(la, la_pad if padded else None, t_chunk)
        bcum = _dot_f32_rhs(tril_b, la_p)
        b_last = bcum[t_chunk - 1:t_chunk, :]
        kk = k_ref[rows, :]
        q_t = (q_ref[rows, :] * scale * jnp.exp(bcum)).astype(BF16)
        k_t = _pad_rows(kk * jnp.exp(-bcum), kt_pad if padded else None, t_chunk).astype(BF16)
        k_w = _pad_rows(kk * jnp.exp(b_last - bcum), kw_pad if padded else None, t_chunk).astype(BF16)
        v_p = _pad_rows(v_ref[rows, :], v_pad if padded else None, t_chunk).astype(BF16)
        outs = []
        for h in range(n_heads):
            ks = slice(h * dk, (h + 1) * dk)
            vs = slice(h * dv, (h + 1) * dv)
            att = jnp.where(tril, _dot(q_t[:, ks], k_t[:, ks], NT_DIMS), 0.0)
            s_prev = s_ref[h]
            o = _dot(att.astype(BF16), v_p[:, vs]) + _dot(q_t[:, ks], s_prev.astype(BF16))
            ds = _dot(k_w[:, ks], v_p[:, vs], TN_DIMS)
            dcol = _dot_f32_lhs(la_p[:, ks], ones_b, TN_DIMS)
            dec = jnp.exp(dcol)
            s_ref[h] = s_prev * jnp.concatenate([dec] * (dv // LANES), axis=1) + ds
            ms = jnp.mean(o * o, axis=-1, keepdims=True)
            outs.append((o * lax.rsqrt(ms + EPS)) * gnorm_ref[:, vs])
        y = jnp.concatenate(outs, axis=1) * _silu(r_ref[rows, :])
        y_ref[rows, :] = y.astype(y_ref.dtype)

    @pl.when(c == pl.num_programs(1) - 1)
    def _fin():
        sout_ref[...] = s_ref[...]


def _gla(qkvr, dt_g, s0, consts, *, n_seq, n_steps, t_rows, t_chunk, dims, rows3d, out_dtype):
    n_heads, dk, dv = dims
    kd, vd = n_heads * dk, n_heads * dv
    assert dk % LANES == 0 and dv % LANES == 0 and vd % kd == 0 and t_rows % t_chunk == 0
    if rows3d:
        def spec(width, blk):
            return pl.BlockSpec((None, t_rows, width), lambda b, c: (b, 0, blk))
        y_shape = (n_seq, t_rows, vd)
        y_spec = pl.BlockSpec((None, t_rows, vd), lambda b, c: (b, 0, 0))
    else:
        def spec(width, blk):
            return pl.BlockSpec((t_rows, width), lambda b, c: (b * n_steps + c, blk))
        y_shape = (n_seq * n_steps * t_rows, vd)
        y_spec = pl.BlockSpec((t_rows, vd), lambda b, c: (b * n_steps + c, 0))
    wa, ba, gnorm = consts
    in_specs = [spec(kd, 0), spec(kd, 1), spec(vd, (2 * kd) // vd), spec(vd, (2 * kd) // vd + 1),
                spec(LANES, 1), _full(wa.shape), _full(ba.shape), _full(gnorm.shape),
                pl.BlockSpec((None, n_heads, dk, dv), lambda b, c: (b, 0, 0, 0))]
    scratch = [pltpu.VMEM((n_heads, dk, dv), F32)]
    if t_chunk < GLA_CHUNK:
        scratch += [pltpu.VMEM((GLA_CHUNK, kd), F32), pltpu.VMEM((GLA_CHUNK, kd), F32),
                    pltpu.VMEM((GLA_CHUNK, kd), F32), pltpu.VMEM((GLA_CHUNK, vd), F32)]
    kern = functools.partial(_gla_kernel, t_rows=t_rows, t_chunk=t_chunk, n_heads=n_heads, dk=dk, dv=dv)
    return pl.pallas_call(
        kern,
        grid=(n_seq, n_steps),
        in_specs=in_specs,
        out_specs=[y_spec, pl.BlockSpec((None, n_heads, dk, dv), lambda b, c: (b, 0, 0, 0))],
        out_shape=[jax.ShapeDtypeStruct(y_shape, out_dtype),
                   jax.ShapeDtypeStruct((n_seq, n_heads, dk, dv), F32)],
        scratch_shapes=scratch,
        compiler_params=_params("arbitrary", "arbitrary"),
    )(qkvr, qkvr, qkvr, qkvr, dt_g, wa, ba, gnorm, s0)


def _cmix_kernel(u_ref, v_ref, w_ref, bsx_ref, lng_ref, lnb_ref, y_ref, vn_ref, *, n_groups, gd, n_ptiles):
    i = pl.program_id(0)
    for g in range(n_groups):
        cols = slice(g * gd, (g + 1) * gd)
        v = v_ref[:, cols]
        mu = jnp.mean(v, axis=-1, keepdims=True)
        d = v - mu
        var = jnp.mean(d * d, axis=-1, keepdims=True)
        vn = (d * lax.rsqrt(var + EPS)) * lng_ref[:, cols] + lnb_ref[:, cols]
        mixed = _dot(w_ref[g].astype(BF16), vn.astype(BF16)) + bsx_ref[g]
        y_ref[:, cols] = (u_ref[:, cols] * mixed).astype(y_ref.dtype)

        @pl.when(i >= n_ptiles)
        def _keep():
            vn_ref[:, cols] = vn


def _cmix(uv, w_sel, bsx_sel, ln_g, ln_b, *, rows_p, chunk, n_groups, gd):
    n_rows = uv.shape[0]
    cw = n_groups * gd
    n_ptiles = rows_p // chunk
    kern = functools.partial(_cmix_kernel, n_groups=n_groups, gd=gd, n_ptiles=n_ptiles)
    sel = lambda i: (jnp.where(i >= n_ptiles, 1, 0), 0, 0, 0)
    return pl.pallas_call(
        kern,
        grid=(n_rows // chunk,),
        in_specs=[pl.BlockSpec((chunk, cw), lambda i: (i, 0)), pl.BlockSpec((chunk, cw), lambda i: (i, 1)),
                  pl.BlockSpec((None, n_groups, chunk, chunk), sel),
                  pl.BlockSpec((None, n_groups, chunk, gd), sel),
                  _full((1, cw)), _full((1, cw))],
        out_specs=[pl.BlockSpec((chunk, cw), lambda i: (i, 0)),
                   pl.BlockSpec((chunk, cw), lambda i: (jnp.maximum(i - n_ptiles, 0), 0))],
        out_shape=[jax.ShapeDtypeStruct((n_rows, cw), BF16),
                   jax.ShapeDtypeStruct((n_rows - rows_p, cw), F32)],
        compiler_params=_params("arbitrary"),
    )(uv, uv, w_sel, bsx_sel, ln_g.reshape(1, cw), ln_b.reshape(1, cw))


def _weights_changed(te_ref, t):
    return (t == 0) | (te_ref[t] != te_ref[jnp.maximum(t - 1, 0)])


def _ffn_up_kernel(te_ref, x_ref, w1_ref, w3_ref, o_ref, w1b_ref, w3b_ref):
    t = pl.program_id(1)

    @pl.when(_weights_changed(te_ref, t))
    def _cast():
        w1b_ref[...] = w1_ref[...].astype(BF16)
        w3b_ref[...] = w3_ref[...].astype(BF16)

    x = x_ref[...]
    o_ref[...] = (_silu(_dot(x, w1b_ref[...])) * _dot(x, w3b_ref[...])).astype(o_ref.dtype)


def _ffn_up(x, w1, w3, tile_expert, *, tm, tf):
    n_rows, d = x.shape
    f = w1.shape[2]
    grid = (pl.cdiv(f, tf), n_rows // tm)
    return pl.pallas_call(
        _ffn_up_kernel,
        grid_spec=pltpu.PrefetchScalarGridSpec(
            num_scalar_prefetch=1,
            grid=grid,
            in_specs=[pl.BlockSpec((tm, d), lambda j, t, te: (t, 0)),
                      pl.BlockSpec((None, d, tf), lambda j, t, te: (te[t], 0, j)),
                      pl.BlockSpec((None, d, tf), lambda j, t, te: (te[t], 0, j))],
            out_specs=pl.BlockSpec((tm, tf), lambda j, t, te: (t, j)),
            scratch_shapes=[pltpu.VMEM((d, tf), BF16), pltpu.VMEM((d, tf), BF16)]),
        out_shape=jax.ShapeDtypeStruct((n_rows, f), BF16),
        compiler_params=_params("arbitrary", "arbitrary"),
    )(tile_expert, x, w1, w3)


def _ffn_down_kernel(te_ref, g_ref, w2_ref, o_ref, w2b_ref):
    t = pl.program_id(1)

    @pl.when(_weights_changed(te_ref, t))
    def _cast():
        w2b_ref[...] = w2_ref[...].astype(BF16)

    o_ref[...] = _dot(g_ref[...], w2b_ref[...])


def _ffn_down(g, w2, tile_expert, *, tm, tn):
    n_rows, f = g.shape
    d = w2.shape[2]
    return pl.pallas_call(
        _ffn_down_kernel,
        grid_spec=pltpu.PrefetchScalarGridSpec(
            num_scalar_prefetch=1,
            grid=(d // tn, n_rows // tm),
            in_specs=[pl.BlockSpec((tm, f), lambda j, t, te: (t, 0)),
                      pl.BlockSpec((None, f, tn), lambda j, t, te: (te[t], 0, j))],
            out_specs=pl.BlockSpec((tm, tn), lambda j, t, te: (t, j)),
            scratch_shapes=[pltpu.VMEM((f, tn), BF16)]),
        out_shape=jax.ShapeDtypeStruct((n_rows, d), F32),
        compiler_params=_params("arbitrary", "arbitrary"),
    )(tile_expert, g, w2)


def _router_kernel(h_ref, w_ref, idx_ref, gate_ref, *, n_exp):
    logits = jnp.dot(h_ref[...], w_ref[...], preferred_element_type=F32, precision=lax.Precision.HIGHEST)
    lane = _iota(logits.shape, 1)
    neg = jnp.float32(-jnp.inf)
    l1 = jnp.where(lane < n_exp, logits, neg)
    m1 = jnp.max(l1, axis=-1, keepdims=True)
    i1 = jnp.min(jnp.where(l1 == m1, lane, LANES), axis=-1, keepdims=True)
    l2 = jnp.where(lane == i1, neg, l1)
    m2 = jnp.max(l2, axis=-1, keepdims=True)
    i2 = jnp.min(jnp.where(l2 == m2, lane, LANES), axis=-1, keepdims=True)
    e = jnp.exp(m2 - m1)
    g1 = 1.0 / (1.0 + e)
    g2 = e / (1.0 + e)
    idx_ref[...] = jnp.where(lane == 0, i1, jnp.where(lane == 1, i2, 0))
    gate_ref[...] = jnp.where(lane == 0, g1, jnp.where(lane == 1, g2, 0.0))


def _router(h, w_pad, *, n_exp, tm):
    n_rows, d = h.shape
    kern = functools.partial(_router_kernel, n_exp=n_exp)
    return pl.pallas_call(
        kern,
        grid=(n_rows // tm,),
        in_specs=[pl.BlockSpec((tm, d), lambda i: (i, 0)), _full(w_pad.shape)],
        out_specs=[pl.BlockSpec((tm, LANES), lambda i: (i, 0)), pl.BlockSpec((tm, LANES), lambda i: (i, 0))],
        out_shape=[jax.ShapeDtypeStruct((n_rows, LANES), I32), jax.ShapeDtypeStruct((n_rows, LANES), F32)],
        compiler_params=_params("arbitrary"),
    )(h, w_pad)


def _row_copy(src_hbm, row, dst_ref, slot, sem):
    return pltpu.make_async_copy(src_hbm.at[pl.ds(row, 1), :], dst_ref.at[pl.ds(slot, 1), :], sem)


def _gather_kernel(idx_ref, src_hbm, o_ref, buf_ref, sem, *, tm):
    def _start(r, carry):
        _row_copy(src_hbm, idx_ref[0, 0, r], buf_ref, r, sem).start()
        return carry

    def _wait(r, carry):
        _row_copy(src_hbm, idx_ref[0, 0, r], buf_ref, r, sem).wait()
        return carry

    lax.fori_loop(0, tm, _start, 0)
    lax.fori_loop(0, tm, _wait, 0)
    o_ref[...] = buf_ref[...].astype(o_ref.dtype)


def _gather_rows(src, row_idx, *, tm, out_dtype):
    n_out = row_idx.shape[0]
    d = src.shape[1]
    n_tiles = n_out // tm
    kern = functools.partial(_gather_kernel, tm=tm)
    return pl.pallas_call(
        kern,
        grid=(n_tiles,),
        in_specs=[pl.BlockSpec((1, 1, tm), lambda i: (i, 0, 0), memory_space=pltpu.SMEM),
                  pl.BlockSpec(memory_space=pl.ANY)],
        out_specs=pl.BlockSpec((tm, d), lambda i: (i, 0)),
        out_shape=jax.ShapeDtypeStruct((n_out, d), out_dtype),
        scratch_shapes=[pltpu.VMEM((tm, d), src.dtype), pltpu.SemaphoreType.DMA(())],
        compiler_params=_params("arbitrary"),
    )(row_idx.reshape(n_tiles, 1, tm), src)


def _combine_kernel(pos_ref, ys_hbm, g0_ref, g1_ref, x_ref, gp_ref, gs_ref, gain_ref, o_ref,
                    b0_ref, b1_ref, sem, *, tm, n_ptiles):
    i = pl.program_id(0)

    def _start(r, carry):
        _row_copy(ys_hbm, pos_ref[0, 0, r], b0_ref, r, sem).start()
        _row_copy(ys_hbm, pos_ref[0, 0, tm + r], b1_ref, r, sem).start()
        return carry

    def _wait(r, carry):
        _row_copy(ys_hbm, pos_ref[0, 0, r], b0_ref, r, sem).wait()
        _row_copy(ys_hbm, pos_ref[0, 0, tm + r], b1_ref, r, sem).wait()
        return carry

    lax.fori_loop(0, tm, _start, 0)
    lax.fori_loop(0, tm, _wait, 0)
    y = g0_ref[...] * b0_ref[...] + g1_ref[...] * b1_ref[...]
    x = x_ref[...] + jnp.where(i < n_ptiles, gp_ref[...], gs_ref[...]) * y
    ms = jnp.mean(x * x, axis=-1, keepdims=True)
    o_ref[...] = (x * lax.rsqrt(ms + EPS)) * gain_ref[...]


def _combine(ys, pos, g0, g1, x, gate_p, gate_s, gain, *, rows_p, len_p, tm):
    n_rows, d = x.shape
    n_tiles = n_rows // tm
    n_ptiles = rows_p // tm
    tiles_per_batch = len_p // tm
    n_bp = rows_p // len_p
    pos_t = jnp.concatenate([pos[:, 0].reshape(n_tiles, 1, tm), pos[:, 1].reshape(n_tiles, 1, tm)], axis=2)
    row_spec = pl.BlockSpec((tm, d), lambda i: (i, 0))
    col_spec = pl.BlockSpec((tm, 1), lambda i: (i, 0))
    kern = functools.partial(_combine_kernel, tm=tm, n_ptiles=n_ptiles)
    return pl.pallas_call(
        kern,
        grid=(n_tiles,),
        in_specs=[pl.BlockSpec((1, 1, 2 * tm), lambda i: (i, 0, 0), memory_space=pltpu.SMEM),
                  pl.BlockSpec(memory_space=pl.ANY), col_spec, col_spec, row_spec,
                  pl.BlockSpec((None, 1, d), lambda i: (jnp.minimum(i // tiles_per_batch, n_bp - 1), 0, 0)),
                  pl.BlockSpec((tm, d), lambda i: (jnp.maximum(i - n_ptiles, 0), 0)),
                  _full((1, d))],
        out_specs=row_spec,
        out_shape=jax.ShapeDtypeStruct((n_rows, d), F32),
        scratch_shapes=[pltpu.VMEM((tm, d), F32), pltpu.VMEM((tm, d), F32), pltpu.SemaphoreType.DMA(())],
        compiler_params=_params("arbitrary"),
    )(pos_t, ys, g0, g1, x, gate_p, gate_s, gain.reshape(1, d))


def _routing_tables(top_idx, n_exp, tm, n_tiles):
    n_tok = top_idx.shape[0]
    flat_e = top_idx.reshape(-1)
    order = jnp.argsort(flat_e, stable=True).astype(I32)
    sorted_e = flat_e[order]
    counts = jnp.zeros((n_exp,), I32).at[flat_e].add(1)
    padded = ((counts + tm - 1) // tm) * tm
    start = jnp.cumsum(counts) - counts
    start_p = jnp.cumsum(padded) - padded
    rank = jnp.arange(TOP_K * n_tok, dtype=I32) - start[sorted_e]
    dest = start_p[sorted_e] + rank
    row_src = jnp.zeros((n_tiles * tm,), I32).at[dest].set(order // TOP_K)
    pos = jnp.zeros((TOP_K * n_tok,), I32).at[order].set(dest).reshape(n_tok, TOP_K)
    tile_end = jnp.cumsum(padded) // tm
    tile_expert = jnp.minimum(jnp.searchsorted(tile_end, jnp.arange(n_tiles, dtype=I32), side="right"),
                              n_exp - 1).astype(I32)
    return row_src, pos, tile_expert


def _pick_tile(n, cands):
    for c in cands:
        if n % c == 0:
            return c
    raise ValueError(f"no tile for {n}")


def kernel(x_prompt, x_sample, state_ssm, state_conv, state_gla, c_prompt, c_sample, ada_w0, ada_b0, norm_mix0, norm_ffn0, w_in0, conv_w, conv_b, dt_bias, a_log, d_skip, a_norm, gla_wa2, gla_ba, gla_norm, w_out0, ffn_w1, ffn_w3, ffn_w2, ada_w1, ada_b1, norm_mix1, norm_ffn1, c_w_in, c_ln_g, c_ln_b, c_ws, c_bs, c_w_out, router_w, moe_w1, moe_w3, moe_w2, norm_f):
    bp, lp, d = x_prompt.shape
    bs, ls, _ = x_sample.shape
    n_even = state_ssm.shape[0]
    assert n_even == 1 and ada_w1.shape[0] == 1
    _, _, n_heads, head_dim, n_state = state_ssm.shape
    conv_taps, conv_dim = conv_w.shape[1], conv_w.shape[2]
    hp = n_heads * head_dim
    gn = (conv_dim - hp) // 2
    n_groups = gn // n_state
    _, _, g_heads, dk, dv = state_gla.shape
    kd, vd = g_heads * dk, g_heads * dv
    rank = gla_wa2.shape[1]
    c_groups, c_chunk = c_ws.shape[1], c_ws.shape[2]
    c_width = c_w_out.shape[1]
    gd = c_width // c_groups
    n_exp, d_ff = moe_w1.shape[1], moe_w1.shape[3]
    rows_p, rows_s = bp * lp, bs * ls
    n_rows = rows_p + rows_s
    assert lp % SSD_CHUNK == 0 and lp % c_chunk == 0 and c_chunk % ls == 0 and ls <= GLA_CHUNK
    assert ls >= conv_taps - 1 and conv_taps - 1 <= SUBLANES and rank <= LANES

    tm_norm = _pick_tile(math.gcd(lp, rows_s), (256, 128, 64, 32, 16, 8))
    tm_mm = _pick_tile(n_rows, (1088, 1024, 512, 256, 128, 64, 32, 16, 8))
    tm_dual = rows_s
    assert rows_p % tm_dual == 0
    tm_ffn = _pick_tile(n_rows, (512, 256, 128, 64, 32, 16, 8))

    def pick_tn(n, k=d):
        fits = [t for t in (1024, 512, 256, 128) if k * t * (2 * 4 + 2) <= V7X_VMEM_BYTES // 3]
        return _pick_tile(n, fits)

    c_all = jnp.concatenate([c_prompt, c_sample], axis=0)

    def ada(w, b):
        mod = _matmul([c_all], w, tm=c_all.shape[0], tn=pick_tn(w.shape[1]), bias=b, lhs_act="silu")
        out = []
        for k in range(6):
            m = mod[:, k * d:(k + 1) * d]
            out.append((m[:bp].reshape(bp, 1, d), jnp.repeat(m[bp:], ls, axis=0)))
        return out

    x0 = jnp.concatenate([x_prompt.reshape(rows_p, d), x_sample.reshape(rows_s, d)], axis=0)
    norm_kw = dict(rows_p=rows_p, len_p=lp, tm=tm_norm)

    sm, scm, gm, sf, scf, gf = ada(ada_w0[0], ada_b0[0])
    h = _norm(x0, norm_mix0[0], mod=(sm[0], sm[1], scm[0], scm[1]), **norm_kw)

    w_in = w_in0[0]
    off_dt = hp + conv_dim
    off_q = off_dt + n_heads
    off_g = off_q + 2 * kd + 2 * vd
    assert off_dt % LANES == 0
    zxbc = _matmul([h], w_in, tm=tm_mm, tn=pick_tn(off_dt), n_cols=off_dt)
    qkvr = _matmul([h], w_in[:, off_q:off_g], tm=tm_mm, tn=pick_tn(off_g - off_q))
    w_small = jnp.concatenate([jnp.pad(w_in[:, off_dt:off_q], ((0, 0), (0, LANES - n_heads))),
                               jnp.pad(w_in[:, off_g:], ((0, 0), (0, LANES - rank)))], axis=1)
    dt_g = _matmul([h], w_small, tm=tm_mm, tn=2 * LANES)

    head_of_col = jnp.arange(hp, dtype=I32) // head_dim
    e64 = (jnp.arange(LANES, dtype=I32)[:, None] == head_of_col[None, :]).astype(BF16)
    e128 = (jnp.arange(LANES, dtype=I32)[:, None]
            == (jnp.arange(LANES * n_heads, dtype=I32) // LANES)[None, :]).astype(BF16)
    ssd_consts = (conv_w[0], conv_b[0].reshape(1, conv_dim),
                  jnp.pad(dt_bias[0], (0, LANES - n_heads)).reshape(1, LANES),
                  jnp.pad(a_log[0], (0, LANES - n_heads)).reshape(1, LANES),
                  jnp.repeat(d_skip[0], head_dim).reshape(1, hp), a_norm[0].reshape(1, hp), e64, e64.T, e128)
    ssd_dims = (n_heads, head_dim, n_groups, n_state, conv_taps)
    ya_p, ssm_p = _ssd(zxbc, dt_g, jnp.zeros((bp, SUBLANES, conv_dim), F32), jnp.zeros((bp, hp, n_state), F32),
                       ssd_consts, n_seq=bp, n_chunks=lp // SSD_CHUNK, t_rows=SSD_CHUNK, dims=ssd_dims,
                       rows3d=False, out_dtype=BF16)
    zxbc_s = zxbc[rows_p:].reshape(bs, ls, off_dt)
    dt_g_s = dt_g[rows_p:].reshape(bs, ls, 2 * LANES)
    conv_init_s = jnp.pad(state_conv[0], ((0, 0), (SUBLANES - (conv_taps - 1), 0), (0, 0)))
    ya_s, ssm_s = _ssd(zxbc_s, dt_g_s, conv_init_s, state_ssm[0].reshape(bs, hp, n_state), ssd_consts,
                       n_seq=bs, n_chunks=1, t_rows=ls, dims=ssd_dims, rows3d=True, out_dtype=F32)

    gla_consts = (jnp.pad(gla_wa2[0], ((0, LANES - rank), (0, 0))), gla_ba[0].reshape(1, kd),
                  gla_norm[0].reshape(1, vd))
    gla_dims = (g_heads, dk, dv)
    t_gla = 2 * GLA_CHUNK
    yb_p, gla_p = _gla(qkvr, dt_g, jnp.zeros((bp, g_heads, dk, dv), F32), gla_consts, n_seq=bp,
                       n_steps=lp // t_gla, t_rows=t_gla, t_chunk=GLA_CHUNK, dims=gla_dims, rows3d=False,
                       out_dtype=BF16)
    qkvr_s = qkvr[rows_p:].reshape(bs, ls, off_g - off_q)
    yb_s, gla_s = _gla(qkvr_s, dt_g_s, state_gla[0], gla_consts, n_seq=bs, n_steps=1, t_rows=ls, t_chunk=ls,
                       dims=gla_dims, rows3d=True, out_dtype=F32)

    ya_s2 = ya_s.reshape(rows_s, hp).astype(BF16)
    yb_s2 = yb_s.reshape(rows_s, vd).astype(BF16)
    y_mix = _matmul([(ya_p, ya_s2), (yb_p, yb_s2)], w_out0[0], tm=tm_dual, tn=pick_tn(d, hp + vd))
    x1, h = _norm(x0, norm_ffn0[0], resid=(y_mix, gm[0], gm[1]), mod=(sf[0], sf[1], scf[0], scf[1]), **norm_kw)

    dense_tiles = jnp.zeros((n_rows // tm_ffn,), I32)
    tf = 512 if d_ff > 512 else d_ff
    g_act = _ffn_up(h, ffn_w1, ffn_w3, dense_tiles, tm=tm_ffn, tf=tf)
    y_ffn = _ffn_down(g_act, ffn_w2, dense_tiles, tm=tm_ffn, tn=pick_tn(d, d_ff))

    sm, scm, gm1, sf, scf, gf1 = ada(ada_w1[0], ada_b1[0])
    x2, h = _norm(x1, norm_mix1[0], resid=(y_ffn, gf[0], gf[1]), mod=(sm[0], sm[1], scm[0], scm[1]), **norm_kw)
    uv = _matmul([h], c_w_in[0], tm=tm_mm, tn=pick_tn(2 * c_width), act="gelu")
    tril_ws = jnp.tril(c_ws[0])
    w_short = jnp.tril(c_ws[0][:, :ls, :ls])
    eye = jnp.eye(c_chunk // ls, dtype=F32)
    w_kron = jnp.einsum("ab,gts->gatbs", eye, w_short).reshape(c_groups, c_chunk, c_chunk)
    w_sel = jnp.stack([tril_ws, w_kron])
    bs_p = jnp.broadcast_to(c_bs[0][:, :, None], (c_groups, c_chunk, gd))
    bs_s = jnp.broadcast_to(jnp.tile(c_bs[0][:, :ls], (1, c_chunk // ls))[:, :, None], (c_groups, c_chunk, gd))
    y_c, vn_s = _cmix(uv, w_sel, jnp.stack([bs_p, bs_s]), c_ln_g[0], c_ln_b[0], rows_p=rows_p, chunk=c_chunk,
                      n_groups=c_groups, gd=gd)
    y_mix1 = _matmul([y_c], c_w_out[0], tm=tm_mm, tn=pick_tn(d, c_width))
    x3, h32 = _norm(x2, norm_ffn1[0], resid=(y_mix1, gm1[0], gm1[1]), mod=(sf[0], sf[1], scf[0], scf[1]),
                    h_dtype=F32, **norm_kw)

    idx_l, gate_l = _router(h32, jnp.pad(router_w[0], ((0, 0), (0, LANES - n_exp))), n_exp=n_exp, tm=tm_ffn)
    top_idx, gates = idx_l[:, :TOP_K], gate_l[:, :TOP_K]
    n_moe_tiles = (TOP_K * n_rows + n_exp * (tm_ffn - 1)) // tm_ffn
    row_src, pos, tile_expert = _routing_tables(top_idx, n_exp, tm_ffn, n_moe_tiles)
    xs = _gather_rows(h32, row_src, tm=tm_ffn, out_dtype=BF16)
    g_moe = _ffn_up(xs, moe_w1[0], moe_w3[0], tile_expert, tm=tm_ffn, tf=tf)
    ys = _ffn_down(g_moe, moe_w2[0], tile_expert, tm=tm_ffn, tn=pick_tn(d, d_ff))
    y_out = _combine(ys, pos, gates[:, 0:1], gates[:, 1:2], x3, gf1[0], gf1[1], norm_f, rows_p=rows_p,
                     len_p=lp, tm=tm_norm)

    y_prompt = y_out[:rows_p].reshape(bp, lp, d)
    y_sample = y_out[rows_p:].reshape(bs, ls, d)
    xbc_p = zxbc[:rows_p, hp:].reshape(bp, lp, conv_dim)
    conv_prompt = xbc_p[:, lp - (conv_taps - 1):][None]
    conv_full_s = jnp.concatenate([state_conv[0], zxbc_s[:, :, hp:]], axis=1)
    conv_sample = conv_full_s[:, conv_full_s.shape[1] - (conv_taps - 1):][None]
    ssm_prompt = ssm_p.reshape(1, bp, n_heads, head_dim, n_state)
    ssm_sample = ssm_s.reshape(1, bs, n_heads, head_dim, n_state)
    gla_prompt = gla_p[None]
    gla_sample = gla_s[None]
    cmlp_v_sample = vn_s.reshape(1, bs, ls, c_groups, gd)
    return (y_prompt, y_sample, ssm_prompt, conv_prompt, gla_prompt, ssm_sample, conv_sample, gla_sample,
            cmlp_v_sample)
```

```python
import functools
import math
from typing import NamedTuple

import jax
import jax.numpy as jnp
from jax import lax
from jax.experimental import pallas as pl
from jax.experimental.pallas import tpu as pltpu

F32 = jnp.float32
BF16 = jnp.bfloat16
I32 = jnp.int32

EPS = 1e-6
LANES = 128
SUBLANES = 8
BF16_ROWS = 16
SSD_CHUNK = 128
GLA_CHUNK = 64
GLA_GATE_TAU = 16.0
TOP_K = 2
V7X_VMEM_BYTES = 64 * 1024 * 1024
VMEM_LIMIT = V7X_VMEM_BYTES - 8 * 1024 * 1024
VMEM_BUDGET = VMEM_LIMIT - 8 * 1024 * 1024

NT_DIMS = (((1,), (1,)), ((), ()))
TN_DIMS = (((0,), (0,)), ((), ()))


def _sigmoid(x):
    return 1.0 / (1.0 + jnp.exp(-x))


def _silu(x):
    return x * _sigmoid(x)


def _softplus(x):
    return jnp.maximum(x, 0.0) + jnp.log1p(jnp.exp(-jnp.abs(x)))


def _gelu_tanh(x):
    c = math.sqrt(2.0 / math.pi)
    return x * (0.5 * (1.0 + jnp.tanh(c * (x + 0.044715 * (x * x * x)))))


def _dot(a, b, dims=None):
    if dims is None:
        return jnp.dot(a, b, preferred_element_type=F32)
    return lax.dot_general(a, b, dims, preferred_element_type=F32)


def _split3(x):
    hi = x.astype(BF16)
    r1 = x - hi.astype(F32)
    mid = r1.astype(BF16)
    lo = (r1 - mid.astype(F32)).astype(BF16)
    return hi, mid, lo


def _dot_f32_lhs(a_f32, b_exact, dims=None):
    p0, p1, p2 = _split3(a_f32)
    return _dot(p0, b_exact, dims) + _dot(p1, b_exact, dims) + _dot(p2, b_exact, dims)


def _dot_f32_rhs(a_exact, b_f32, dims=None):
    p0, p1, p2 = _split3(b_f32)
    return _dot(a_exact, p0, dims) + _dot(a_exact, p1, dims) + _dot(a_exact, p2, dims)


def _iota(shape, axis):
    return lax.broadcasted_iota(I32, shape, axis)


def _params(*sem):
    return pltpu.CompilerParams(dimension_semantics=sem, vmem_limit_bytes=VMEM_LIMIT)


def _full(shape):
    nd = len(shape)
    return pl.BlockSpec(shape, lambda *_: (0,) * nd)


def _pick_tile(n, cands):
    for c in cands:
        if n % c == 0:
            return c
    raise ValueError(f"no tile for {n}")


def _fit_tile(n, est_bytes, cands=(2048, 1024, 512, 256, 128)):
    for t in cands:
        if n % t == 0 and est_bytes(t) <= VMEM_BUDGET:
            return t
    raise ValueError(f"no tile of {n} fits VMEM")


class _Rows(NamedTuple):
    rows_p: int
    rows_s: int
    len_p: int
    tm: int

    @property
    def n_ptiles(self):
        return self.rows_p // self.tm

    @property
    def n_tiles(self):
        return (self.rows_p + self.rows_s) // self.tm

    def row(self, d):
        return pl.BlockSpec((self.tm, d), lambda i: (i, 0))

    def prompt_row(self, d):
        last = self.n_ptiles - 1
        return pl.BlockSpec((self.tm, d), lambda i: (jnp.minimum(i, last), 0))

    def sample_row(self, d, col=0):
        n_pt = self.n_ptiles
        return pl.BlockSpec((self.tm, d), lambda i: (jnp.maximum(i - n_pt, 0), col))

    def prompt_mod(self, d, col):
        per_batch = self.len_p // self.tm
        last = self.rows_p // self.len_p - 1
        return pl.BlockSpec((None, 1, d), lambda i: (jnp.minimum(i // per_batch, last), 0, col))


def _mm_kernel(*refs, n_lhs, k_sizes, n_ptiles, dual, lhs_act, has_bias, act, lane_shift):
    pos = 0
    lhs = []
    for _ in range(n_lhs):
        if dual:
            lhs.append((refs[pos], refs[pos + 1]))
            pos += 2
        else:
            lhs.append((refs[pos], None))
            pos += 1
    w_ref = refs[pos]
    pos += 1
    wx_ref = None
    if lane_shift:
        wx_ref = refs[pos]
        pos += 1
    b_ref = None
    if has_bias:
        b_ref = refs[pos]
        pos += 1
    o_ref, wb_ref = refs[pos], refs[pos + 1]
    i = pl.program_id(1)

    @pl.when(i == 0)
    def _cast():
        if lane_shift:
            tn = w_ref.shape[1]
            w = jnp.concatenate([w_ref[...], wx_ref[...]], axis=1)
            wb_ref[...] = w[:, lane_shift:lane_shift + tn].astype(BF16)
        else:
            wb_ref[...] = w_ref[...].astype(BF16)

    acc = None
    k0 = 0
    for (xp_ref, xs_ref), kk in zip(lhs, k_sizes):
        x = xp_ref[...]
        if xs_ref is not None:
            x = jnp.where(i < n_ptiles, x, xs_ref[...])
        if lhs_act == "silu":
            x = _silu(x.astype(F32))
        part = _dot(x.astype(BF16), wb_ref[k0:k0 + kk, :])
        acc = part if acc is None else acc + part
        k0 += kk
    if has_bias:
        acc = acc + b_ref[...]
    if act == "gelu":
        acc = _gelu_tanh(acc)
    o_ref[...] = acc.astype(o_ref.dtype)


def _matmul(lhs, w, *, tm, n_cols=None, col0=0, bias=None, lhs_act=None, act=None, out_dtype=F32):
    dual = isinstance(lhs[0], tuple)
    k_sizes = tuple((p[0] if dual else p).shape[1] for p in lhs)
    k_total = sum(k_sizes)
    assert w.shape[0] == k_total
    if dual:
        rows_p, rows_s = lhs[0][0].shape[0], lhs[0][1].shape[0]
        assert rows_p % tm == 0 and rows_s == tm
        n_ptiles = rows_p // tm
        n_rows = rows_p + rows_s
    else:
        n_rows = lhs[0].shape[0]
        assert n_rows % tm == 0
        n_ptiles = n_rows // tm
    n_cols = w.shape[1] if n_cols is None else n_cols
    lhs_bytes = jnp.dtype((lhs[0][0] if dual else lhs[0]).dtype).itemsize * (2 if dual else 1)
    out_bytes = jnp.dtype(out_dtype).itemsize
    lane_shift = col0 % LANES
    col_base = col0 - lane_shift
    tn = _fit_tile(math.gcd(n_cols, col_base) if col_base else n_cols,
                   lambda t: k_total * t * (2 * 4 + 2) + 2 * tm * k_total * lhs_bytes + 2 * tm * t * out_bytes)
    assert n_cols % tn == 0 and col_base % tn == 0
    cb0 = col_base // tn
    grid = (n_cols // tn, n_rows // tm)

    args, in_specs = [], []
    for part, kk in zip(lhs, k_sizes):
        if dual:
            args += [part[0], part[1]]
            in_specs += [pl.BlockSpec((tm, kk), lambda j, i: (jnp.minimum(i, n_ptiles - 1), 0)),
                         pl.BlockSpec((tm, kk), lambda j, i: (0, 0))]
        else:
            args.append(part)
            in_specs.append(pl.BlockSpec((tm, kk), lambda j, i: (i, 0)))
    args.append(w)
    in_specs.append(pl.BlockSpec((k_total, tn), lambda j, i: (0, j + cb0)))
    if lane_shift:
        assert bias is None
        per_tile = tn // LANES
        args.append(w)
        in_specs.append(pl.BlockSpec((k_total, LANES), lambda j, i: (0, (j + cb0 + 1) * per_tile)))
    if bias is not None:
        args.append(bias.reshape(1, -1))
        in_specs.append(pl.BlockSpec((1, tn), lambda j, i: (0, j + cb0)))
    kern = functools.partial(_mm_kernel, n_lhs=len(lhs), k_sizes=k_sizes, n_ptiles=n_ptiles, dual=dual,
                             lhs_act=lhs_act, has_bias=bias is not None, act=act, lane_shift=lane_shift)
    return pl.pallas_call(
        kern,
        grid=grid,
        in_specs=in_specs,
        out_specs=pl.BlockSpec((tm, tn), lambda j, i: (i, j)),
        out_shape=jax.ShapeDtypeStruct((n_rows, n_cols), out_dtype),
        scratch_shapes=[pltpu.VMEM((k_total, tn), BF16)],
        compiler_params=_params("arbitrary", "arbitrary"),
    )(*args)


def _norm_kernel(*refs, n_ptiles, split_x, has_resid, has_mod):
    refs = list(refs)
    xp_ref = refs.pop(0)
    xs_ref = refs.pop(0) if split_x else xp_ref
    if has_resid:
        y_ref, gp_ref, gs_ref = refs.pop(0), refs.pop(0), refs.pop(0)
    gain_ref = refs.pop(0)
    if has_mod:
        shp_ref, shs_ref, scp_ref, scs_ref = refs.pop(0), refs.pop(0), refs.pop(0), refs.pop(0)
    xo_ref = refs.pop(0) if has_resid else None
    h_ref = refs.pop(0)

    def body(x_ref, gate_ref, shift_ref, scale_ref):
        x = x_ref[...]
        if has_resid:
            x = x + gate_ref[...] * y_ref[...]
            xo_ref[...] = x
        ms = jnp.mean(x * x, axis=-1, keepdims=True)
        h = (x * lax.rsqrt(ms + EPS)) * gain_ref[...]
        if has_mod:
            h = h * (1.0 + scale_ref[...]) + shift_ref[...]
        h_ref[...] = h.astype(h_ref.dtype)

    is_prompt = pl.program_id(0) < n_ptiles

    @pl.when(is_prompt)
    def _prompt():
        body(xp_ref, gp_ref if has_resid else None, shp_ref if has_mod else None, scp_ref if has_mod else None)

    @pl.when(jnp.logical_not(is_prompt))
    def _sample():
        body(xs_ref, gs_ref if has_resid else None, shs_ref if has_mod else None, scs_ref if has_mod else None)


def _norm(x, gain, plan, mods, *, resid=None, mod_cols=None, h_dtype=BF16):
    split_x = isinstance(x, tuple)
    d = gain.shape[-1]
    n_rows = plan.rows_p + plan.rows_s
    mod_p, mod_s = mods
    args, in_specs = [], []
    if split_x:
        args += [x[0], x[1]]
        in_specs += [plan.prompt_row(d), plan.sample_row(d)]
    else:
        args.append(x)
        in_specs.append(plan.row(d))
    if resid is not None:
        y, (gate_p, gate_s), gate_col = resid
        args += [y, gate_p, gate_s]
        in_specs += [plan.row(d), plan.prompt_mod(d, gate_col), plan.sample_row(d, gate_col)]
    args.append(gain.reshape(1, d))
    in_specs.append(_full((1, d)))
    if mod_cols is not None:
        for col in mod_cols:
            args += [mod_p, mod_s]
            in_specs += [plan.prompt_mod(d, col), plan.sample_row(d, col)]
    out_shape = [jax.ShapeDtypeStruct((n_rows, d), h_dtype)]
    out_specs = [plan.row(d)]
    if resid is not None:
        out_shape.insert(0, jax.ShapeDtypeStruct((n_rows, d), F32))
        out_specs.insert(0, plan.row(d))
    kern = functools.partial(_norm_kernel, n_ptiles=plan.n_ptiles, split_x=split_x, has_resid=resid is not None,
                             has_mod=mod_cols is not None)
    out = pl.pallas_call(
        kern,
        grid=(plan.n_tiles,),
        in_specs=in_specs,
        out_specs=out_specs,
        out_shape=out_shape,
        compiler_params=_params("arbitrary"),
    )(*args)
    return tuple(out) if resid is not None else out[0]


def _pad_rows(val, pad_ref, n_rows):
    if pad_ref is None:
        return val
    pad_ref[0:n_rows, :] = val
    return pad_ref[...]


def _ssd_kernel(z_ref, x_ref, bc_ref, dt_ref, cinit_ref, cw_ref, cb_ref, dtb_ref, alog_ref, dexp_ref,
                anorm_ref, e64_ref, e64t_ref, e128_ref, h0_ref, y_ref, hout_ref, h_ref, xpad_ref,
                *pad_refs, t_rows, n_heads, head_dim, n_groups, n_state, conv_w):
    c = pl.program_id(1)
    tp = SSD_CHUNK
    hp = n_heads * head_dim
    gn = n_groups * n_state
    heads_per_group = n_heads // n_groups
    group_w = heads_per_group * head_dim
    padded = t_rows < tp
    if padded:
        da_pad, xdt_pad, b_pad, xw_pad = pad_refs

    @pl.when(c == 0)
    def _init():
        h_ref[...] = h0_ref[...]
        xpad_ref[0:SUBLANES, :] = cinit_ref[...]
        for r in pad_refs:
            r[...] = jnp.zeros(r.shape, r.dtype)

    xpad_ref[SUBLANES:SUBLANES + t_rows, 0:hp] = x_ref[...].astype(F32)
    xpad_ref[SUBLANES:SUBLANES + t_rows, hp:] = bc_ref[...].astype(F32)
    first = SUBLANES - conv_w + 1
    acc = cb_ref[...] + cw_ref[0:1, :] * xpad_ref[first:first + t_rows, :]
    for k in range(1, conv_w):
        acc = acc + cw_ref[k:k + 1, :] * xpad_ref[first + k:first + k + t_rows, :]
    xc = _silu(acc)
    if not padded:
        xpad_ref[0:SUBLANES, :] = xpad_ref[t_rows:t_rows + SUBLANES, :]
    xs = xc[:, :hp]
    bm = xc[:, hp:hp + gn]
    cm = xc[:, hp + gn:]

    lane_row = _iota((1, LANES), 1)
    dt = _softplus(dt_ref[...] + dtb_ref[...])
    a = jnp.where(lane_row < n_heads, -jnp.exp(alog_ref[...]), 0.0)
    da_p = _pad_rows(dt * a, da_pad if padded else None, t_rows)
    tril = (_iota((tp, tp), 1) <= _iota((tp, tp), 0)).astype(BF16)
    acum_p = _dot_f32_rhs(tril, da_p)
    acum = acum_p[0:t_rows, :]
    acum_t = acum_p.T
    dt_exp = _dot_f32_lhs(dt, e64_ref[...])
    ac_exp = _dot_f32_lhs(acum, e64_ref[...])
    al_exp = ac_exp[t_rows - 1:t_rows, :]
    col = _dot_f32_lhs(acum, e128_ref[...])

    xdt = xs * dt_exp
    xdt_p = _pad_rows(xdt, xdt_pad if padded else None, t_rows).astype(BF16)
    b_p = _pad_rows(bm, b_pad if padded else None, t_rows).astype(BF16)
    c_b = cm.astype(BF16)
    xw = xdt * jnp.exp(al_exp - ac_exp)
    xw_p = _pad_rows(xw, xw_pad if padded else None, t_rows).astype(BF16)
    h_b = h_ref[...].astype(BF16)

    causal = _iota((t_rows, tp), 1) <= _iota((t_rows, tp), 0)
    lane_lo = _iota((tp, LANES), 1) < head_dim
    y_diag, y_off, new_states = [], [], []
    for g in range(n_groups):
        c_g = c_b[:, g * n_state:(g + 1) * n_state]
        b_g = b_p[:, g * n_state:(g + 1) * n_state]
        cb = _dot(c_g, b_g, NT_DIMS)
        for pj in range(heads_per_group // 2):
            h1 = g * heads_per_group + 2 * pj
            ms = []
            for hh in (h1, h1 + 1):
                seg = col[:, hh * LANES:(hh + 1) * LANES] - acum_t[hh:hh + 1, :]
                ms.append(jnp.where(causal, jnp.exp(seg), 0.0) * cb)
            lhs = jnp.concatenate(ms, axis=1).astype(BF16)
            xp = xdt_p[:, h1 * head_dim:h1 * head_dim + LANES]
            rhs = jnp.concatenate([jnp.where(lane_lo, xp, jnp.zeros_like(xp)),
                                   jnp.where(lane_lo, jnp.zeros_like(xp), xp)], axis=0)
            y_diag.append(_dot(lhs, rhs))
        h_g = h_b[g * group_w:(g + 1) * group_w, :]
        y_off.append(_dot(c_g, h_g, NT_DIMS))
        new_states.append(_dot(xw_p[:, g * group_w:(g + 1) * group_w], b_g, TN_DIMS))

    y = (jnp.concatenate(y_diag, axis=1) + jnp.concatenate(y_off, axis=1) * jnp.exp(ac_exp)
         + dexp_ref[...] * xs)

    sel = (_iota((tp, LANES), 0) == tp - 1).astype(BF16)
    a_last_t = _dot_f32_lhs(acum_t, sel)
    chunk_decay = jnp.exp(_dot_f32_rhs(e64t_ref[...], a_last_t))
    h_ref[...] = h_ref[...] * chunk_decay + jnp.concatenate(new_states, axis=0)

    yz = y * _silu(z_ref[...].astype(F32))
    ms = jnp.mean(yz * yz, axis=-1, keepdims=True)
    y_ref[...] = ((yz * lax.rsqrt(ms + EPS)) * anorm_ref[...]).astype(y_ref.dtype)

    @pl.when(c == pl.num_programs(1) - 1)
    def _fin():
        hout_ref[...] = h_ref[...]


def _ssd(z_x_bc, dt_g, conv_init, h0, consts, *, n_seq, n_chunks, t_rows, dims, rows3d, out_dtype):
    n_heads, head_dim, n_groups, n_state, conv_w = dims
    hp, gn = n_heads * head_dim, n_groups * n_state
    assert 2 * head_dim == LANES and n_state == LANES and n_heads <= LANES
    assert (n_heads // n_groups) % 2 == 0 and hp % (2 * gn) == 0 and hp % LANES == 0
    bc_blk = (2 * hp) // (2 * gn)
    if rows3d:
        def spec(width, blk):
            return pl.BlockSpec((None, t_rows, width), lambda b, c: (b, 0, blk))
        y_shape = (n_seq, t_rows, hp)
        y_spec = pl.BlockSpec((None, t_rows, hp), lambda b, c: (b, 0, 0))
    else:
        def spec(width, blk):
            return pl.BlockSpec((t_rows, width), lambda b, c: (b * n_chunks + c, blk))
        y_shape = (n_seq * n_chunks * t_rows, hp)
        y_spec = pl.BlockSpec((t_rows, hp), lambda b, c: (b * n_chunks + c, 0))
    cw, cb, dtb, alog, dexp, anorm, e64, e64t, e128 = consts
    cdim = hp + 2 * gn
    state_spec = pl.BlockSpec((None, hp, n_state), lambda b, c: (b, 0, 0))
    in_specs = [spec(hp, 0), spec(hp, 1), spec(2 * gn, bc_blk), spec(LANES, 0),
                pl.BlockSpec((None, SUBLANES, cdim), lambda b, c: (b, 0, 0)),
                _full(cw.shape), _full(cb.shape), _full(dtb.shape), _full(alog.shape), _full(dexp.shape),
                _full(anorm.shape), _full(e64.shape), _full(e64t.shape), _full(e128.shape), state_spec]
    scratch = [pltpu.VMEM((hp, n_state), F32), pltpu.VMEM((SUBLANES + max(t_rows, SUBLANES), cdim), F32)]
    if t_rows < SSD_CHUNK:
        scratch += [pltpu.VMEM((SSD_CHUNK, LANES), F32), pltpu.VMEM((SSD_CHUNK, hp), F32),
                    pltpu.VMEM((SSD_CHUNK, gn), F32), pltpu.VMEM((SSD_CHUNK, hp), F32)]
    kern = functools.partial(_ssd_kernel, t_rows=t_rows, n_heads=n_heads, head_dim=head_dim,
                             n_groups=n_groups, n_state=n_state, conv_w=conv_w)
    return pl.pallas_call(
        kern,
        grid=(n_seq, n_chunks),
        in_specs=in_specs,
        out_specs=[y_spec, state_spec],
        out_shape=[jax.ShapeDtypeStruct(y_shape, out_dtype),
                   jax.ShapeDtypeStruct((n_seq, hp, n_state), F32)],
        scratch_shapes=scratch,
        compiler_params=_params("arbitrary", "arbitrary"),
    )(z_x_bc, z_x_bc, z_x_bc, dt_g, conv_init, cw, cb, dtb, alog, dexp, anorm, e64, e64t, e128, h0)


def _gla_kernel(q_ref, k_ref, v_ref, r_ref, g_ref, wa_ref, ba_ref, gnorm_ref, s0_ref, y_ref, sout_ref,
                s_ref, *pad_refs, t_rows, t_chunk, n_heads, dk, dv):
    c = pl.program_id(1)
    tp = GLA_CHUNK
    padded = t_chunk < tp
    if padded:
        la_pad, kt_pad, kw_pad, v_pad = pad_refs

    @pl.when(c == 0)
    def _init():
        s_ref[...] = s0_ref[...]
        for r in pad_refs:
            r[...] = jnp.zeros(r.shape, r.dtype)

    wa_b = wa_ref[...].astype(BF16)
    tril = (_iota((t_chunk, tp), 1) <= _iota((t_chunk, tp), 0))
    tril_b = tril.astype(BF16)
    ones_b = jnp.ones((tp, LANES), BF16)
    scale = dk ** -0.5
    for sub in range(t_rows // t_chunk):
        rows = slice(sub * t_chunk, (sub + 1) * t_chunk)
        la_raw = _dot(g_ref[rows, :].astype(BF16), wa_b) + ba_ref[...]
        la = -_softplus(-la_raw) * (1.0 / GLA_GATE_TAU)
        la_p = _pad_rows(la, la_pad if padded else None, t_chunk)
        bcum = _dot_f32_rhs(tril_b, la_p)
        b_last = bcum[t_chunk - 1:t_chunk, :]
        kk = k_ref[rows, :].astype(F32)
        q_t = (q_ref[rows, :].astype(F32) * scale * jnp.exp(bcum)).astype(BF16)
        k_t = _pad_rows(kk * jnp.exp(-bcum), kt_pad if padded else None, t_chunk).astype(BF16)
        k_w = _pad_rows(kk * jnp.exp(b_last - bcum), kw_pad if padded else None, t_chunk).astype(BF16)
        v_p = _pad_rows(v_ref[rows, :].astype(F32), v_pad if padded else None, t_chunk).astype(BF16)
        outs = []
        for h in range(n_heads):
            ks = slice(h * dk, (h + 1) * dk)
            vs = slice(h * dv, (h + 1) * dv)
            att = jnp.where(tril, _dot(q_t[:, ks], k_t[:, ks], NT_DIMS), 0.0)
            s_prev = s_ref[h]
            o = _dot(att.astype(BF16), v_p[:, vs]) + _dot(q_t[:, ks], s_prev.astype(BF16))
            ds = _dot(k_w[:, ks], v_p[:, vs], TN_DIMS)
            dcol = _dot_f32_lhs(la_p[:, ks], ones_b, TN_DIMS)
            dec = jnp.exp(dcol)
            s_ref[h] = s_prev * jnp.concatenate([dec] * (dv // LANES), axis=1) + ds
            ms = jnp.mean(o * o, axis=-1, keepdims=True)
            outs.append((o * lax.rsqrt(ms + EPS)) * gnorm_ref[:, vs])
        y = jnp.concatenate(outs, axis=1) * _silu(r_ref[rows, :].astype(F32))
        y_ref[rows, :] = y.astype(y_ref.dtype)

    @pl.when(c == pl.num_programs(1) - 1)
    def _fin():
        sout_ref[...] = s_ref[...]


def _gla(qkvr, dt_g, s0, consts, *, n_seq, n_steps, t_rows, t_chunk, dims, rows3d, out_dtype):
    n_heads, dk, dv = dims
    kd, vd = n_heads * dk, n_heads * dv
    assert dk % LANES == 0 and dv % LANES == 0 and vd % kd == 0 and t_rows % t_chunk == 0
    if rows3d:
        def spec(width, blk):
            return pl.BlockSpec((None, t_rows, width), lambda b, c: (b, 0, blk))
        y_shape = (n_seq, t_rows, vd)
        y_spec = pl.BlockSpec((None, t_rows, vd), lambda b, c: (b, 0, 0))
    else:
        def spec(width, blk):
            return pl.BlockSpec((t_rows, width), lambda b, c: (b * n_steps + c, blk))
        y_shape = (n_seq * n_steps * t_rows, vd)
        y_spec = pl.BlockSpec((t_rows, vd), lambda b, c: (b * n_steps + c, 0))
    wa, ba, gnorm = consts
    state_spec = pl.BlockSpec((None, n_heads, dk, dv), lambda b, c: (b, 0, 0, 0))
    in_specs = [spec(kd, 0), spec(kd, 1), spec(vd, (2 * kd) // vd), spec(vd, (2 * kd) // vd + 1),
                spec(LANES, 1), _full(wa.shape), _full(ba.shape), _full(gnorm.shape), state_spec]
    scratch = [pltpu.VMEM((n_heads, dk, dv), F32)]
    if t_chunk < GLA_CHUNK:
        scratch += [pltpu.VMEM((GLA_CHUNK, kd), F32), pltpu.VMEM((GLA_CHUNK, kd), F32),
                    pltpu.VMEM((GLA_CHUNK, kd), F32), pltpu.VMEM((GLA_CHUNK, vd), F32)]
    kern = functools.partial(_gla_kernel, t_rows=t_rows, t_chunk=t_chunk, n_heads=n_heads, dk=dk, dv=dv)
    return pl.pallas_call(
        kern,
        grid=(n_seq, n_steps),
        in_specs=in_specs,
        out_specs=[y_spec, state_spec],
        out_shape=[jax.ShapeDtypeStruct(y_shape, out_dtype),
                   jax.ShapeDtypeStruct((n_seq, n_heads, dk, dv), F32)],
        scratch_shapes=scratch,
        compiler_params=_params("arbitrary", "arbitrary"),
    )(qkvr, qkvr, qkvr, qkvr, dt_g, wa, ba, gnorm, s0)


def _cmix_kernel(u_ref, v_ref, w_ref, bsx_ref, lng_ref, lnb_ref, y_ref, vn_ref, *, n_groups, gd, n_ptiles):
    i = pl.program_id(0)
    for g in range(n_groups):
        cols = slice(g * gd, (g + 1) * gd)
        v = v_ref[:, cols].astype(F32)
        mu = jnp.mean(v, axis=-1, keepdims=True)
        d = v - mu
        var = jnp.mean(d * d, axis=-1, keepdims=True)
        vn = (d * lax.rsqrt(var + EPS)) * lng_ref[:, cols] + lnb_ref[:, cols]
        mixed = _dot(w_ref[g].astype(BF16), vn.astype(BF16)) + bsx_ref[g]
        y_ref[:, cols] = (u_ref[:, cols].astype(F32) * mixed).astype(y_ref.dtype)

        @pl.when(i >= n_ptiles)
        def _keep():
            vn_ref[:, cols] = vn


def _cmix(uv, w_sel, bsx_sel, ln_g, ln_b, *, rows_p, chunk, n_groups, gd):
    n_rows = uv.shape[0]
    cw = n_groups * gd
    n_ptiles = rows_p // chunk
    kern = functools.partial(_cmix_kernel, n_groups=n_groups, gd=gd, n_ptiles=n_ptiles)
    sel = lambda i: (jnp.where(i >= n_ptiles, 1, 0), 0, 0, 0)
    return pl.pallas_call(
        kern,
        grid=(n_rows // chunk,),
        in_specs=[pl.BlockSpec((chunk, cw), lambda i: (i, 0)), pl.BlockSpec((chunk, cw), lambda i: (i, 1)),
                  pl.BlockSpec((None, n_groups, chunk, chunk), sel),
                  pl.BlockSpec((None, n_groups, chunk, gd), sel),
                  _full((1, cw)), _full((1, cw))],
        out_specs=[pl.BlockSpec((chunk, cw), lambda i: (i, 0)),
                   pl.BlockSpec((chunk, cw), lambda i: (jnp.maximum(i - n_ptiles, 0), 0))],
        out_shape=[jax.ShapeDtypeStruct((n_rows, cw), BF16),
                   jax.ShapeDtypeStruct((n_rows - rows_p, cw), F32)],
        compiler_params=_params("arbitrary"),
    )(uv, uv, w_sel, bsx_sel, ln_g.reshape(1, cw), ln_b.reshape(1, cw))


def _weights_changed(te_ref, t):
    return (t == 0) | (te_ref[t] != te_ref[jnp.maximum(t - 1, 0)])


def _ffn_up_kernel(te_ref, nu_ref, x_ref, w1_ref, w3_ref, o_ref, w1b_ref, w3b_ref):
    t = pl.program_id(1)

    @pl.when(_weights_changed(te_ref, t))
    def _cast():
        w1b_ref[...] = w1_ref[...].astype(BF16)
        w3b_ref[...] = w3_ref[...].astype(BF16)

    @pl.when(t < nu_ref[0])
    def _compute():
        x = x_ref[...]
        o_ref[...] = (_silu(_dot(x, w1b_ref[...])) * _dot(x, w3b_ref[...])).astype(o_ref.dtype)

    @pl.when(t >= nu_ref[0])
    def _unused():
        o_ref[...] = jnp.zeros(o_ref.shape, o_ref.dtype)


def _ffn_up(x, w1, w3, tile_expert, n_used, *, tm, tf):
    n_rows, d = x.shape
    f = w1.shape[2]
    grid = (pl.cdiv(f, tf), n_rows // tm)
    return pl.pallas_call(
        _ffn_up_kernel,
        grid_spec=pltpu.PrefetchScalarGridSpec(
            num_scalar_prefetch=2,
            grid=grid,
            in_specs=[pl.BlockSpec((tm, d), lambda j, t, te, nu: (t, 0)),
                      pl.BlockSpec((None, d, tf), lambda j, t, te, nu: (te[t], 0, j)),
                      pl.BlockSpec((None, d, tf), lambda j, t, te, nu: (te[t], 0, j))],
            out_specs=pl.BlockSpec((tm, tf), lambda j, t, te, nu: (t, j)),
            scratch_shapes=[pltpu.VMEM((d, tf), BF16), pltpu.VMEM((d, tf), BF16)]),
        out_shape=jax.ShapeDtypeStruct((n_rows, f), BF16),
        compiler_params=_params("arbitrary", "arbitrary"),
    )(tile_expert, n_used, x, w1, w3)


def _ffn_down_kernel(te_ref, nu_ref, g_ref, w2_ref, o_ref, w2b_ref):
    t = pl.program_id(1)

    @pl.when(_weights_changed(te_ref, t))
    def _cast():
        w2b_ref[...] = w2_ref[...].astype(BF16)

    @pl.when(t < nu_ref[0])
    def _compute():
        o_ref[...] = _dot(g_ref[...], w2b_ref[...])

    @pl.when(t >= nu_ref[0])
    def _unused():
        o_ref[...] = jnp.zeros(o_ref.shape, o_ref.dtype)


def _ffn_down(g, w2, tile_expert, n_used, *, tm):
    n_rows, f = g.shape
    d = w2.shape[2]
    tn = _fit_tile(d, lambda t: f * t * (2 * 4 + 2) + 2 * tm * f * 2 + 2 * tm * t * 4)
    return pl.pallas_call(
        _ffn_down_kernel,
        grid_spec=pltpu.PrefetchScalarGridSpec(
            num_scalar_prefetch=2,
            grid=(d // tn, n_rows // tm),
            in_specs=[pl.BlockSpec((tm, f), lambda j, t, te, nu: (t, 0)),
                      pl.BlockSpec((None, f, tn), lambda j, t, te, nu: (te[t], 0, j))],
            out_specs=pl.BlockSpec((tm, tn), lambda j, t, te, nu: (t, j)),
            scratch_shapes=[pltpu.VMEM((f, tn), BF16)]),
        out_shape=jax.ShapeDtypeStruct((n_rows, d), F32),
        compiler_params=_params("arbitrary", "arbitrary"),
    )(tile_expert, n_used, g, w2)


def _router_kernel(h_ref, w_ref, idx_ref, gate_ref, *, n_exp):
    logits = jnp.dot(h_ref[...], w_ref[...], preferred_element_type=F32, precision=lax.Precision.HIGHEST)
    lane = _iota(logits.shape, 1)
    neg = jnp.float32(-jnp.inf)
    l1 = jnp.where(lane < n_exp, logits, neg)
    m1 = jnp.max(l1, axis=-1, keepdims=True)
    i1 = jnp.min(jnp.where(l1 == m1, lane, LANES), axis=-1, keepdims=True)
    l2 = jnp.where(lane == i1, neg, l1)
    m2 = jnp.max(l2, axis=-1, keepdims=True)
    i2 = jnp.min(jnp.where(l2 == m2, lane, LANES), axis=-1, keepdims=True)
    e = jnp.exp(m2 - m1)
    g1 = 1.0 / (1.0 + e)
    g2 = e / (1.0 + e)
    idx_ref[...] = jnp.where(lane == 0, i1, jnp.where(lane == 1, i2, 0))
    gate_ref[...] = jnp.where(lane == 0, g1, jnp.where(lane == 1, g2, 0.0))


def _router(h, w_pad, *, n_exp, tm):
    n_rows, d = h.shape
    kern = functools.partial(_router_kernel, n_exp=n_exp)
    return pl.pallas_call(
        kern,
        grid=(n_rows // tm,),
        in_specs=[pl.BlockSpec((tm, d), lambda i: (i, 0)), _full(w_pad.shape)],
        out_specs=[pl.BlockSpec((tm, LANES), lambda i: (i, 0)), pl.BlockSpec((tm, LANES), lambda i: (i, 0))],
        out_shape=[jax.ShapeDtypeStruct((n_rows, LANES), I32), jax.ShapeDtypeStruct((n_rows, LANES), F32)],
        compiler_params=_params("arbitrary"),
    )(h, w_pad)


def _token_copy(src_hbm, tok, dst_hbm, row, sem):
    return pltpu.make_async_copy(src_hbm.at[tok], dst_hbm.at[row], sem)


def _gather_kernel(idx_ref, src_hbm, dst_hbm, sem, *, tm):
    base = pl.program_id(0) * tm

    def _start(r, carry):
        _token_copy(src_hbm, idx_ref[0, 0, r], dst_hbm, base + r, sem).start()
        return carry

    def _wait(r, carry):
        _token_copy(src_hbm, idx_ref[0, 0, r], dst_hbm, base + r, sem).wait()
        return carry

    lax.fori_loop(0, tm, _start, 0)
    lax.fori_loop(0, tm, _wait, 0)


def _gather_tokens(src, row_idx, *, tm):
    n_out = row_idx.shape[0]
    n_tiles = n_out // tm
    kern = functools.partial(_gather_kernel, tm=tm)
    return pl.pallas_call(
        kern,
        grid=(n_tiles,),
        in_specs=[pl.BlockSpec((1, 1, tm), lambda i: (i, 0, 0), memory_space=pltpu.SMEM),
                  pl.BlockSpec(memory_space=pl.ANY)],
        out_specs=pl.BlockSpec(memory_space=pl.ANY),
        out_shape=jax.ShapeDtypeStruct((n_out,) + src.shape[1:], src.dtype),
        scratch_shapes=[pltpu.SemaphoreType.DMA(())],
        compiler_params=_params("arbitrary"),
    )(row_idx.reshape(n_tiles, 1, tm), src)


def _row_copy(src_hbm, row, dst_ref, slot, sem):
    return pltpu.make_async_copy(src_hbm.at[pl.ds(row, 1), :], dst_ref.at[pl.ds(slot, 1), :], sem)


def _combine_kernel(pos_ref, ys_hbm, g0_ref, g1_ref, x_ref, gp_ref, gs_ref, gain_ref, op_ref, os_ref,
                    b0_ref, b1_ref, sem, *, tm, n_ptiles):
    i = pl.program_id(0)

    def _start(r, carry):
        _row_copy(ys_hbm, pos_ref[0, 0, r], b0_ref, r, sem).start()
        _row_copy(ys_hbm, pos_ref[0, 0, tm + r], b1_ref, r, sem).start()
        return carry

    def _wait(r, carry):
        _row_copy(ys_hbm, pos_ref[0, 0, r], b0_ref, r, sem).wait()
        _row_copy(ys_hbm, pos_ref[0, 0, tm + r], b1_ref, r, sem).wait()
        return carry

    lax.fori_loop(0, tm, _start, 0)
    lax.fori_loop(0, tm, _wait, 0)
    def finish(gate_ref, o_ref):
        y = g0_ref[...] * b0_ref[...] + g1_ref[...] * b1_ref[...]
        x = x_ref[...] + gate_ref[...] * y
        ms = jnp.mean(x * x, axis=-1, keepdims=True)
        o_ref[...] = (x * lax.rsqrt(ms + EPS)) * gain_ref[...]

    @pl.when(i < n_ptiles)
    def _prompt():
        finish(gp_ref, op_ref)

    @pl.when(i >= n_ptiles)
    def _sample():
        finish(gs_ref, os_ref)


def _combine(ys, pos, g0, g1, x, plan, mods, gate_col, gain):
    d = x.shape[1]
    tm = plan.tm
    n_tiles = plan.n_tiles
    mod_p, mod_s = mods
    pos_t = jnp.concatenate([pos[:, 0].reshape(n_tiles, 1, tm), pos[:, 1].reshape(n_tiles, 1, tm)], axis=2)
    col_spec = pl.BlockSpec((tm, 1), lambda i: (i, 0))
    kern = functools.partial(_combine_kernel, tm=tm, n_ptiles=plan.n_ptiles)
    return pl.pallas_call(
        kern,
        grid=(n_tiles,),
        in_specs=[pl.BlockSpec((1, 1, 2 * tm), lambda i: (i, 0, 0), memory_space=pltpu.SMEM),
                  pl.BlockSpec(memory_space=pl.ANY), col_spec, col_spec, plan.row(d),
                  plan.prompt_mod(d, gate_col), plan.sample_row(d, gate_col), _full((1, d))],
        out_specs=[plan.prompt_row(d), plan.sample_row(d)],
        out_shape=[jax.ShapeDtypeStruct((plan.rows_p, d), F32), jax.ShapeDtypeStruct((plan.rows_s, d), F32)],
        scratch_shapes=[pltpu.VMEM((tm, d), F32), pltpu.VMEM((tm, d), F32), pltpu.SemaphoreType.DMA(())],
        compiler_params=_params("arbitrary"),
    )(pos_t, ys, g0, g1, x, mod_p, mod_s, gain.reshape(1, d))


def _routing_tables(top_idx, n_exp, tm, n_tiles):
    n_tok = top_idx.shape[0]
    n_slots = TOP_K * n_tok
    flat_e = top_idx.reshape(-1)
    order = jnp.argsort(flat_e, stable=True).astype(I32)
    inv_order = jnp.argsort(order).astype(I32)
    counts = jnp.sum((flat_e[:, None] == jnp.arange(n_exp, dtype=I32)[None, :]).astype(I32), axis=0)
    padded = ((counts + tm - 1) // tm) * tm
    start = jnp.cumsum(counts) - counts
    end_p = jnp.cumsum(padded)
    start_p = end_p - padded
    shift = start_p - start
    pos = (inv_order + shift[flat_e]).reshape(n_tok, TOP_K)
    tile_expert = jnp.minimum(jnp.searchsorted(end_p // tm, jnp.arange(n_tiles, dtype=I32), side="right"),
                              n_exp - 1).astype(I32)
    row = jnp.arange(n_tiles * tm, dtype=I32)
    row_e = jnp.repeat(tile_expert, tm)
    sorted_idx = row - shift[row_e]
    valid = (row - start_p[row_e]) < counts[row_e]
    row_src = jnp.where(valid, order[jnp.clip(sorted_idx, 0, n_slots - 1)] // TOP_K, 0).astype(I32)
    n_used = (end_p[n_exp - 1:] // tm).astype(I32)
    return row_src, pos, tile_expert, n_used


def kernel(x_prompt, x_sample, state_ssm, state_conv, state_gla, c_prompt, c_sample, ada_w0, ada_b0, norm_mix0, norm_ffn0, w_in0, conv_w, conv_b, dt_bias, a_log, d_skip, a_norm, gla_wa2, gla_ba, gla_norm, w_out0, ffn_w1, ffn_w3, ffn_w2, ada_w1, ada_b1, norm_mix1, norm_ffn1, c_w_in, c_ln_g, c_ln_b, c_ws, c_bs, c_w_out, router_w, moe_w1, moe_w3, moe_w2, norm_f):
    bp, lp, d = x_prompt.shape
    bs, ls, _ = x_sample.shape
    n_even = state_ssm.shape[0]
    assert n_even == 1 and ada_w1.shape[0] == 1
    _, _, n_heads, head_dim, n_state = state_ssm.shape
    conv_taps, conv_dim = conv_w.shape[1], conv_w.shape[2]
    hp = n_heads * head_dim
    gn = (conv_dim - hp) // 2
    n_groups = gn // n_state
    _, _, g_heads, dk, dv = state_gla.shape
    kd, vd = g_heads * dk, g_heads * dv
    rank = gla_wa2.shape[1]
    c_groups, c_chunk = c_ws.shape[1], c_ws.shape[2]
    c_width = c_w_out.shape[1]
    gd = c_width // c_groups
    n_exp, d_ff = moe_w1.shape[1], moe_w1.shape[3]
    rows_p, rows_s = bp * lp, bs * ls
    n_rows = rows_p + rows_s
    assert lp % SSD_CHUNK == 0 and lp % c_chunk == 0 and c_chunk % ls == 0 and ls <= GLA_CHUNK
    assert ls >= conv_taps - 1 and conv_taps - 1 <= SUBLANES and rank <= LANES and d % LANES == 0

    plan = _Rows(rows_p, rows_s, lp, _pick_tile(math.gcd(lp, rows_s), (256, 128, 64, 32, 16, 8)))
    tm_mm = _pick_tile(n_rows, (1088, 1024, 512, 256, 128, 64, 32, 16, 8))
    tm_dual = rows_s
    assert rows_p % tm_dual == 0
    tm_ffn = _pick_tile(n_rows, (512, 256, 128, 64, 32, 16, 8))

    c_all = jnp.concatenate([c_prompt, c_sample], axis=0)

    def ada(w, b):
        mod = _matmul([c_all], w, tm=c_all.shape[0], bias=b, lhs_act="silu")
        return mod[:bp].reshape(bp, 1, 6 * d), jnp.repeat(mod[bp:], ls, axis=0)

    x0 = (x_prompt.reshape(rows_p, d), x_sample.reshape(rows_s, d))

    mods0 = ada(ada_w0[0], ada_b0[0])
    h = _norm(x0, norm_mix0[0], plan, mods0, mod_cols=(0, 1))

    w_in = w_in0[0]
    off_dt = hp + conv_dim
    off_q = off_dt + n_heads
    off_g = off_q + 2 * kd + 2 * vd
    assert off_dt % LANES == 0
    zxbc = _matmul([h], w_in, tm=tm_mm, n_cols=off_dt, out_dtype=BF16)
    qkvr = _matmul([h], w_in, tm=tm_mm, col0=off_q, n_cols=off_g - off_q, out_dtype=BF16)
    w_small = jnp.concatenate([jnp.pad(w_in[:, off_dt:off_q], ((0, 0), (0, LANES - n_heads))),
                               jnp.pad(w_in[:, off_g:], ((0, 0), (0, LANES - rank)))], axis=1)
    dt_g = _matmul([h], w_small, tm=tm_mm)

    head_of_col = jnp.arange(hp, dtype=I32) // head_dim
    e64 = (jnp.arange(LANES, dtype=I32)[:, None] == head_of_col[None, :]).astype(BF16)
    e128 = (jnp.arange(LANES, dtype=I32)[:, None]
            == (jnp.arange(LANES * n_heads, dtype=I32) // LANES)[None, :]).astype(BF16)
    ssd_consts = (conv_w[0], conv_b[0].reshape(1, conv_dim),
                  jnp.pad(dt_bias[0], (0, LANES - n_heads)).reshape(1, LANES),
                  jnp.pad(a_log[0], (0, LANES - n_heads)).reshape(1, LANES),
                  jnp.repeat(d_skip[0], head_dim).reshape(1, hp), a_norm[0].reshape(1, hp), e64, e64.T, e128)
    ssd_dims = (n_heads, head_dim, n_groups, n_state, conv_taps)
    ya_p, ssm_p = _ssd(zxbc, dt_g, jnp.zeros((bp, SUBLANES, conv_dim), F32), jnp.zeros((bp, hp, n_state), F32),
                       ssd_consts, n_seq=bp, n_chunks=lp // SSD_CHUNK, t_rows=SSD_CHUNK, dims=ssd_dims,
                       rows3d=False, out_dtype=BF16)
    zxbc_s = zxbc[rows_p:].reshape(bs, ls, off_dt).astype(F32)
    dt_g_s = dt_g[rows_p:].reshape(bs, ls, 2 * LANES)
    conv_init_s = jnp.pad(state_conv[0], ((0, 0), (SUBLANES - (conv_taps - 1), 0), (0, 0)))
    ya_s, ssm_s = _ssd(zxbc_s, dt_g_s, conv_init_s, state_ssm[0].reshape(bs, hp, n_state), ssd_consts,
                       n_seq=bs, n_chunks=1, t_rows=ls, dims=ssd_dims, rows3d=True, out_dtype=F32)

    gla_consts = (jnp.pad(gla_wa2[0], ((0, LANES - rank), (0, 0))), gla_ba[0].reshape(1, kd),
                  gla_norm[0].reshape(1, vd))
    gla_dims = (g_heads, dk, dv)
    t_gla = 2 * GLA_CHUNK
    yb_p, gla_p = _gla(qkvr, dt_g, jnp.zeros((bp, g_heads, dk, dv), F32), gla_consts, n_seq=bp,
                       n_steps=lp // t_gla, t_rows=t_gla, t_chunk=GLA_CHUNK, dims=gla_dims, rows3d=False,
                       out_dtype=BF16)
    qkvr_s = qkvr[rows_p:].reshape(bs, ls, off_g - off_q).astype(F32)
    yb_s, gla_s = _gla(qkvr_s, dt_g_s, state_gla[0], gla_consts, n_seq=bs, n_steps=1, t_rows=ls, t_chunk=ls,
                       dims=gla_dims, rows3d=True, out_dtype=F32)

    ya_s2 = ya_s.reshape(rows_s, hp).astype(BF16)
    yb_s2 = yb_s.reshape(rows_s, vd).astype(BF16)
    y_mix = _matmul([(ya_p, ya_s2), (yb_p, yb_s2)], w_out0[0], tm=tm_dual)
    x1, h = _norm(x0, norm_ffn0[0], plan, mods0, resid=(y_mix, mods0, 2), mod_cols=(3, 4))

    tf = 512 if d_ff > 512 else d_ff
    tm_down = _pick_tile(n_rows, (544, 512, 256, 128, 64, 32, 16, 8))

    def one_expert(tm):
        return jnp.zeros((n_rows // tm,), I32), jnp.full((1,), n_rows // tm, I32)

    g_act = _ffn_up(h, ffn_w1, ffn_w3, *one_expert(tm_mm), tm=tm_mm, tf=tf)
    y_ffn = _ffn_down(g_act, ffn_w2, *one_expert(tm_down), tm=tm_down)

    mods1 = ada(ada_w1[0], ada_b1[0])
    x2, h = _norm(x1, norm_mix1[0], plan, mods1, resid=(y_ffn, mods0, 5), mod_cols=(0, 1))
    uv = _matmul([h], c_w_in[0], tm=tm_mm, act="gelu", out_dtype=BF16)
    tril_ws = jnp.tril(c_ws[0])
    w_short = jnp.tril(c_ws[0][:, :ls, :ls])
    eye = jnp.eye(c_chunk // ls, dtype=F32)
    w_kron = jnp.einsum("ab,gts->gatbs", eye, w_short).reshape(c_groups, c_chunk, c_chunk)
    w_sel = jnp.stack([tril_ws, w_kron])
    bs_p = jnp.broadcast_to(c_bs[0][:, :, None], (c_groups, c_chunk, gd))
    bs_s = jnp.broadcast_to(jnp.tile(c_bs[0][:, :ls], (1, c_chunk // ls))[:, :, None], (c_groups, c_chunk, gd))
    y_c, vn_s = _cmix(uv, w_sel, jnp.stack([bs_p, bs_s]), c_ln_g[0], c_ln_b[0], rows_p=rows_p, chunk=c_chunk,
                      n_groups=c_groups, gd=gd)
    y_mix1 = _matmul([y_c], c_w_out[0], tm=tm_mm)
    x3, h32 = _norm(x2, norm_ffn1[0], plan, mods1, resid=(y_mix1, mods1, 2), mod_cols=(3, 4), h_dtype=F32)

    idx_l, gate_l = _router(h32, jnp.pad(router_w[0], ((0, 0), (0, LANES - n_exp))), n_exp=n_exp, tm=tm_ffn)
    top_idx, gates = idx_l[:, :TOP_K], gate_l[:, :TOP_K]
    n_moe_tiles = (TOP_K * n_rows + n_exp * (tm_ffn - 1)) // tm_ffn
    row_src, pos, tile_expert, n_used = _routing_tables(top_idx, n_exp, tm_ffn, n_moe_tiles)
    assert (d // LANES) in (1, 2, 4, 8) or d % (BF16_ROWS * LANES) == 0
    h_tok = h32.astype(BF16).reshape(n_rows, d // LANES, LANES)
    xs = _gather_tokens(h_tok, row_src, tm=tm_ffn).reshape(n_moe_tiles * tm_ffn, d)
    g_moe = _ffn_up(xs, moe_w1[0], moe_w3[0], tile_expert, n_used, tm=tm_ffn, tf=tf)
    ys = _ffn_down(g_moe, moe_w2[0], tile_expert, n_used, tm=tm_ffn)
    y_p, y_s = _combine(ys, pos, gates[:, 0:1], gates[:, 1:2], x3, plan, mods1, 5, norm_f)

    y_prompt = y_p.reshape(bp, lp, d)
    y_sample = y_s.reshape(bs, ls, d)
    n_hist = conv_taps - 1
    conv_prompt = zxbc[:rows_p].reshape(bp, lp, off_dt)[:, lp - n_hist:, hp:].astype(F32)[None]
    conv_full_s = jnp.concatenate([state_conv[0], zxbc_s[:, :, hp:]], axis=1)
    conv_sample = conv_full_s[:, conv_full_s.shape[1] - n_hist:][None]
    ssm_prompt = ssm_p.reshape(1, bp, n_heads, head_dim, n_state)
    ssm_sample = ssm_s.reshape(1, bs, n_heads, head_dim, n_state)
    gla_prompt = gla_p[None]
    gla_sample = gla_s[None]
    cmlp_v_sample = vn_s.reshape(1, bs, ls, c_groups, gd)
    return (y_prompt, y_sample, ssm_prompt, conv_prompt, gla_prompt, ssm_sample, conv_sample, gla_sample,
            cmlp_v_sample)
```

```python
import functools
import math
from typing import NamedTuple

import jax
import jax.numpy as jnp
from jax import lax
from jax.experimental import pallas as pl
from jax.experimental.pallas import tpu as pltpu

F32 = jnp.float32
BF16 = jnp.bfloat16
I32 = jnp.int32

EPS = 1e-6
LANES = 128
SUBLANES = 8
BF16_ROWS = 16
MXU_COLS = 256
SSD_CHUNK = 128
GLA_CHUNK = 64
GLA_GATE_TAU = 16.0
TOP_K = 2
V7X_VMEM_BYTES = 64 * 1024 * 1024
VMEM_LIMIT = V7X_VMEM_BYTES - 8 * 1024 * 1024
VMEM_BUDGET = VMEM_LIMIT - 8 * 1024 * 1024

NT_DIMS = (((1,), (1,)), ((), ()))
TN_DIMS = (((0,), (0,)), ((), ()))


def _sigmoid(x):
    return 1.0 / (1.0 + jnp.exp(-x))


def _silu(x):
    return x * _sigmoid(x)


def _softplus(x):
    return jnp.maximum(x, 0.0) + jnp.log1p(jnp.exp(-jnp.abs(x)))


def _gelu_tanh(x):
    c = math.sqrt(2.0 / math.pi)
    return x * (0.5 * (1.0 + jnp.tanh(c * (x + 0.044715 * (x * x * x)))))


def _dot(a, b, dims=None):
    if dims is None:
        return jnp.dot(a, b, preferred_element_type=F32)
    return lax.dot_general(a, b, dims, preferred_element_type=F32)


def _split3(x):
    hi = x.astype(BF16)
    r1 = x - hi.astype(F32)
    mid = r1.astype(BF16)
    lo = (r1 - mid.astype(F32)).astype(BF16)
    return hi, mid, lo


def _dot_f32_lhs(a_f32, b_exact, dims=None):
    p0, p1, p2 = _split3(a_f32)
    return _dot(p0, b_exact, dims) + _dot(p1, b_exact, dims) + _dot(p2, b_exact, dims)


def _dot_f32_rhs(a_exact, b_f32, dims=None):
    p0, p1, p2 = _split3(b_f32)
    return _dot(a_exact, p0, dims) + _dot(a_exact, p1, dims) + _dot(a_exact, p2, dims)


def _iota(shape, axis):
    return lax.broadcasted_iota(I32, shape, axis)


def _params(*sem):
    return pltpu.CompilerParams(dimension_semantics=sem, vmem_limit_bytes=VMEM_LIMIT)


def _full(shape):
    nd = len(shape)
    return pl.BlockSpec(shape, lambda *_: (0,) * nd)


def _pick_tile(n, cands):
    for c in cands:
        if n % c == 0:
            return c
    raise ValueError(f"no tile for {n}")


def _fit_tile(n, est_bytes, cands=(2048, 1024, 512, 256, 128)):
    for t in cands:
        if n % t == 0 and est_bytes(t) <= VMEM_BUDGET:
            return t
    raise ValueError(f"no tile of {n} fits VMEM")


class _Rows(NamedTuple):
    rows_p: int
    rows_s: int
    len_p: int
    tm: int

    @property
    def n_ptiles(self):
        return self.rows_p // self.tm

    @property
    def n_tiles(self):
        return (self.rows_p + self.rows_s) // self.tm

    def row(self, d):
        return pl.BlockSpec((self.tm, d), lambda i: (i, 0))

    def prompt_row(self, d):
        last = self.n_ptiles - 1
        return pl.BlockSpec((self.tm, d), lambda i: (jnp.minimum(i, last), 0))

    def sample_row(self, d, col=0):
        n_pt = self.n_ptiles
        return pl.BlockSpec((self.tm, d), lambda i: (jnp.maximum(i - n_pt, 0), col))

    def prompt_mod(self, d, col):
        per_batch = self.len_p // self.tm
        last = self.rows_p // self.len_p - 1
        return pl.BlockSpec((None, 1, d), lambda i: (jnp.minimum(i // per_batch, last), 0, col))


def _mm_kernel(*refs, n_lhs, k_sizes, n_ptiles, dual, lhs_act, has_bias, act, lane_shift):
    pos = 0
    lhs = []
    for _ in range(n_lhs):
        if dual:
            lhs.append((refs[pos], refs[pos + 1]))
            pos += 2
        else:
            lhs.append((refs[pos], None))
            pos += 1
    w_ref = refs[pos]
    pos += 1
    wx_ref = None
    if lane_shift:
        wx_ref = refs[pos]
        pos += 1
    b_ref = None
    if has_bias:
        b_ref = refs[pos]
        pos += 1
    o_ref, wb_ref = refs[pos], refs[pos + 1]
    i = pl.program_id(1)

    @pl.when(i == 0)
    def _cast():
        if lane_shift:
            tn = w_ref.shape[1]
            w = jnp.concatenate([w_ref[...], wx_ref[...]], axis=1)
            wb_ref[...] = w[:, lane_shift:lane_shift + tn].astype(BF16)
        else:
            wb_ref[...] = w_ref[...].astype(BF16)

    xs = []
    for xp_ref, xs_ref in lhs:
        x = xp_ref[...]
        if xs_ref is not None:
            x = jnp.where(i < n_ptiles, x, xs_ref[...])
        if lhs_act == "silu":
            x = _silu(x.astype(F32))
        xs.append(x.astype(BF16))
    tn = o_ref.shape[1]
    cw = MXU_COLS if (act is not None and tn % MXU_COLS == 0) else tn
    for c0 in range(0, tn, cw):
        acc = None
        k0 = 0
        for x, kk in zip(xs, k_sizes):
            part = _dot(x, wb_ref[k0:k0 + kk, c0:c0 + cw])
            acc = part if acc is None else acc + part
            k0 += kk
        if has_bias:
            acc = acc + b_ref[:, c0:c0 + cw]
        if act == "gelu":
            acc = _gelu_tanh(acc)
        o_ref[:, c0:c0 + cw] = acc.astype(o_ref.dtype)


def _matmul(lhs, w, *, tm, n_cols=None, col0=0, bias=None, lhs_act=None, act=None, out_dtype=F32):
    dual = isinstance(lhs[0], tuple)
    k_sizes = tuple((p[0] if dual else p).shape[1] for p in lhs)
    k_total = sum(k_sizes)
    assert w.shape[0] == k_total
    if dual:
        rows_p, rows_s = lhs[0][0].shape[0], lhs[0][1].shape[0]
        assert rows_p % tm == 0 and rows_s == tm
        n_ptiles = rows_p // tm
        n_rows = rows_p + rows_s
    else:
        n_rows = lhs[0].shape[0]
        assert n_rows % tm == 0
        n_ptiles = n_rows // tm
    n_cols = w.shape[1] if n_cols is None else n_cols
    lhs_bytes = jnp.dtype((lhs[0][0] if dual else lhs[0]).dtype).itemsize * (2 if dual else 1)
    out_bytes = jnp.dtype(out_dtype).itemsize
    lane_shift = col0 % LANES
    col_base = col0 - lane_shift
    tn = _fit_tile(math.gcd(n_cols, col_base) if col_base else n_cols,
                   lambda t: k_total * t * (2 * 4 + 2) + 2 * tm * k_total * lhs_bytes + 2 * tm * t * out_bytes)
    assert n_cols % tn == 0 and col_base % tn == 0
    cb0 = col_base // tn
    grid = (n_cols // tn, n_rows // tm)

    args, in_specs = [], []
    for part, kk in zip(lhs, k_sizes):
        if dual:
            args += [part[0], part[1]]
            in_specs += [pl.BlockSpec((tm, kk), lambda j, i: (jnp.minimum(i, n_ptiles - 1), 0)),
                         pl.BlockSpec((tm, kk), lambda j, i: (0, 0))]
        else:
            args.append(part)
            in_specs.append(pl.BlockSpec((tm, kk), lambda j, i: (i, 0)))
    args.append(w)
    in_specs.append(pl.BlockSpec((k_total, tn), lambda j, i: (0, j + cb0)))
    if lane_shift:
        assert bias is None
        per_tile = tn // LANES
        args.append(w)
        in_specs.append(pl.BlockSpec((k_total, LANES), lambda j, i: (0, (j + cb0 + 1) * per_tile)))
    if bias is not None:
        args.append(bias.reshape(1, -1))
        in_specs.append(pl.BlockSpec((1, tn), lambda j, i: (0, j + cb0)))
    kern = functools.partial(_mm_kernel, n_lhs=len(lhs), k_sizes=k_sizes, n_ptiles=n_ptiles, dual=dual,
                             lhs_act=lhs_act, has_bias=bias is not None, act=act, lane_shift=lane_shift)
    return pl.pallas_call(
        kern,
        grid=grid,
        in_specs=in_specs,
        out_specs=pl.BlockSpec((tm, tn), lambda j, i: (i, j)),
        out_shape=jax.ShapeDtypeStruct((n_rows, n_cols), out_dtype),
        scratch_shapes=[pltpu.VMEM((k_total, tn), BF16)],
        compiler_params=_params("arbitrary", "arbitrary"),
    )(*args)


def _norm_kernel(*refs, n_ptiles, split_x, has_resid, has_mod):
    refs = list(refs)
    xp_ref = refs.pop(0)
    xs_ref = refs.pop(0) if split_x else xp_ref
    if has_resid:
        y_ref, gp_ref, gs_ref = refs.pop(0), refs.pop(0), refs.pop(0)
    gain_ref = refs.pop(0)
    if has_mod:
        shp_ref, shs_ref, scp_ref, scs_ref = refs.pop(0), refs.pop(0), refs.pop(0), refs.pop(0)
    xo_ref = refs.pop(0) if has_resid else None
    h_ref = refs.pop(0)

    def body(x_ref, gate_ref, shift_ref, scale_ref):
        x = x_ref[...]
        if has_resid:
            x = x + gate_ref[...] * y_ref[...]
            xo_ref[...] = x
        ms = jnp.mean(x * x, axis=-1, keepdims=True)
        h = (x * lax.rsqrt(ms + EPS)) * gain_ref[...]
        if has_mod:
            h = h * (1.0 + scale_ref[...]) + shift_ref[...]
        h_ref[...] = h.astype(h_ref.dtype)

    is_prompt = pl.program_id(0) < n_ptiles

    @pl.when(is_prompt)
    def _prompt():
        body(xp_ref, gp_ref if has_resid else None, shp_ref if has_mod else None, scp_ref if has_mod else None)

    @pl.when(jnp.logical_not(is_prompt))
    def _sample():
        body(xs_ref, gs_ref if has_resid else None, shs_ref if has_mod else None, scs_ref if has_mod else None)


def _norm(x, gain, plan, mods, *, resid=None, mod_cols=None, h_dtype=BF16):
    split_x = isinstance(x, tuple)
    d = gain.shape[-1]
    n_rows = plan.rows_p + plan.rows_s
    mod_p, mod_s = mods
    args, in_specs = [], []
    if split_x:
        args += [x[0], x[1]]
        in_specs += [plan.prompt_row(d), plan.sample_row(d)]
    else:
        args.append(x)
        in_specs.append(plan.row(d))
    if resid is not None:
        y, (gate_p, gate_s), gate_col = resid
        args += [y, gate_p, gate_s]
        in_specs += [plan.row(d), plan.prompt_mod(d, gate_col), plan.sample_row(d, gate_col)]
    args.append(gain.reshape(1, d))
    in_specs.append(_full((1, d)))
    if mod_cols is not None:
        for col in mod_cols:
            args += [mod_p, mod_s]
            in_specs += [plan.prompt_mod(d, col), plan.sample_row(d, col)]
    out_shape = [jax.ShapeDtypeStruct((n_rows, d), h_dtype)]
    out_specs = [plan.row(d)]
    if resid is not None:
        out_shape.insert(0, jax.ShapeDtypeStruct((n_rows, d), F32))
        out_specs.insert(0, plan.row(d))
    kern = functools.partial(_norm_kernel, n_ptiles=plan.n_ptiles, split_x=split_x, has_resid=resid is not None,
                             has_mod=mod_cols is not None)
    out = pl.pallas_call(
        kern,
        grid=(plan.n_tiles,),
        in_specs=in_specs,
        out_specs=out_specs,
        out_shape=out_shape,
        compiler_params=_params("arbitrary"),
    )(*args)
    return tuple(out) if resid is not None else out[0]


def _pad_rows(val, pad_ref, n_rows):
    if pad_ref is None:
        return val
    pad_ref[0:n_rows, :] = val
    return pad_ref[...]


def _ssd_kernel(*refs, n_sub, **kw):
    if n_sub is None:
        return _ssd_chunk(*refs, **kw)
    seq_major = (0, 1, 2, 3, 4, 14, 15, 16)
    n_fixed = 17
    for s in range(n_sub):
        view = [r.at[s] if (i in seq_major or i >= n_fixed) else r for i, r in enumerate(refs)]
        _ssd_chunk(*view, **kw)


def _ssd_chunk(z_ref, x_ref, bc_ref, dt_ref, cinit_ref, cw_ref, cb_ref, dtb_ref, alog_ref, dexp_ref,
               anorm_ref, e64_ref, e64t_ref, e128_ref, h0_ref, y_ref, hout_ref, h_ref, xpad_ref,
               *pad_refs, t_rows, n_heads, head_dim, n_groups, n_state, conv_w):
    c = pl.program_id(1)
    tp = SSD_CHUNK
    hp = n_heads * head_dim
    gn = n_groups * n_state
    heads_per_group = n_heads // n_groups
    group_w = heads_per_group * head_dim
    padded = t_rows < tp
    if padded:
        da_pad, xdt_pad, b_pad, xw_pad = pad_refs

    @pl.when(c == 0)
    def _init():
        h_ref[...] = h0_ref[...]
        xpad_ref[0:SUBLANES, :] = cinit_ref[...]
        for r in pad_refs:
            r[...] = jnp.zeros(r.shape, r.dtype)

    xpad_ref[SUBLANES:SUBLANES + t_rows, 0:hp] = x_ref[...].astype(F32)
    xpad_ref[SUBLANES:SUBLANES + t_rows, hp:] = bc_ref[...].astype(F32)
    first = SUBLANES - conv_w + 1
    acc = cb_ref[...] + cw_ref[0:1, :] * xpad_ref[first:first + t_rows, :]
    for k in range(1, conv_w):
        acc = acc + cw_ref[k:k + 1, :] * xpad_ref[first + k:first + k + t_rows, :]
    xc = _silu(acc)
    if not padded:
        xpad_ref[0:SUBLANES, :] = xpad_ref[t_rows:t_rows + SUBLANES, :]
    xs = xc[:, :hp]
    bm = xc[:, hp:hp + gn]
    cm = xc[:, hp + gn:]

    lane_row = _iota((1, LANES), 1)
    dt = _softplus(dt_ref[...] + dtb_ref[...])
    a = jnp.where(lane_row < n_heads, -jnp.exp(alog_ref[...]), 0.0)
    da_p = _pad_rows(dt * a, da_pad if padded else None, t_rows)
    tril = (_iota((tp, tp), 1) <= _iota((tp, tp), 0)).astype(BF16)
    acum_p = _dot_f32_rhs(tril, da_p)
    acum = acum_p[0:t_rows, :]
    acum_t = acum_p.T
    dt_exp = _dot_f32_lhs(dt, e64_ref[...])
    ac_exp = _dot_f32_lhs(acum, e64_ref[...])
    al_exp = ac_exp[t_rows - 1:t_rows, :]
    col = _dot_f32_lhs(acum, e128_ref[...])

    xdt = xs * dt_exp
    xdt_p = _pad_rows(xdt, xdt_pad if padded else None, t_rows).astype(BF16)
    b_p = _pad_rows(bm, b_pad if padded else None, t_rows).astype(BF16)
    c_b = cm.astype(BF16)
    xw = xdt * jnp.exp(al_exp - ac_exp)
    xw_p = _pad_rows(xw, xw_pad if padded else None, t_rows).astype(BF16)
    h_b = h_ref[...].astype(BF16)

    causal = _iota((t_rows, tp), 1) <= _iota((t_rows, tp), 0)
    lane_lo = _iota((tp, LANES), 1) < head_dim
    y_diag, y_off, new_states = [], [], []
    for g in range(n_groups):
        c_g = c_b[:, g * n_state:(g + 1) * n_state]
        b_g = b_p[:, g * n_state:(g + 1) * n_state]
        cb = _dot(c_g, b_g, NT_DIMS)
        for pj in range(heads_per_group // 2):
            h1 = g * heads_per_group + 2 * pj
            ms = []
            for hh in (h1, h1 + 1):
                seg = col[:, hh * LANES:(hh + 1) * LANES] - acum_t[hh:hh + 1, :]
                ms.append(jnp.where(causal, jnp.exp(seg), 0.0) * cb)
            lhs = jnp.concatenate(ms, axis=1).astype(BF16)
            xp = xdt_p[:, h1 * head_dim:h1 * head_dim + LANES]
            rhs = jnp.concatenate([jnp.where(lane_lo, xp, jnp.zeros_like(xp)),
                                   jnp.where(lane_lo, jnp.zeros_like(xp), xp)], axis=0)
            y_diag.append(_dot(lhs, rhs))
        h_g = h_b[g * group_w:(g + 1) * group_w, :]
        y_off.append(_dot(c_g, h_g, NT_DIMS))
        new_states.append(_dot(xw_p[:, g * group_w:(g + 1) * group_w], b_g, TN_DIMS))

    y = (jnp.concatenate(y_diag, axis=1) + jnp.concatenate(y_off, axis=1) * jnp.exp(ac_exp)
         + dexp_ref[...] * xs)

    sel = (_iota((tp, LANES), 0) == tp - 1).astype(BF16)
    a_last_t = _dot_f32_lhs(acum_t, sel)
    chunk_decay = jnp.exp(_dot_f32_rhs(e64t_ref[...], a_last_t))
    h_ref[...] = h_ref[...] * chunk_decay + jnp.concatenate(new_states, axis=0)

    yz = y * _silu(z_ref[...].astype(F32))
    ms = jnp.mean(yz * yz, axis=-1, keepdims=True)
    y_ref[...] = ((yz * lax.rsqrt(ms + EPS)) * anorm_ref[...]).astype(y_ref.dtype)

    @pl.when(c == pl.num_programs(1) - 1)
    def _fin():
        hout_ref[...] = h_ref[...]


def _ssd(z_x_bc, dt_g, conv_init, h0, consts, *, n_seq, n_chunks, t_rows, dims, n_sub, out_dtype):
    n_heads, head_dim, n_groups, n_state, conv_w = dims
    hp, gn = n_heads * head_dim, n_groups * n_state
    assert 2 * head_dim == LANES and n_state == LANES and n_heads <= LANES
    assert (n_heads // n_groups) % 2 == 0 and hp % (2 * gn) == 0 and hp % LANES == 0
    bc_blk = (2 * hp) // (2 * gn)
    lead = () if n_sub is None else (n_sub,)
    if n_sub is not None:
        assert n_chunks == 1 and n_seq % n_sub == 0

        def spec(width, blk):
            return pl.BlockSpec((n_sub, t_rows, width), lambda b, c: (b, 0, blk))
        y_shape = (n_seq, t_rows, hp)
        y_spec = pl.BlockSpec((n_sub, t_rows, hp), lambda b, c: (b, 0, 0))
        seq_blk = n_sub
        grid = (n_seq // n_sub, 1)
    else:
        def spec(width, blk):
            return pl.BlockSpec((t_rows, width), lambda b, c: (b * n_chunks + c, blk))
        y_shape = (n_seq * n_chunks * t_rows, hp)
        y_spec = pl.BlockSpec((t_rows, hp), lambda b, c: (b * n_chunks + c, 0))
        seq_blk = None
        grid = (n_seq, n_chunks)
    cw, cb, dtb, alog, dexp, anorm, e64, e64t, e128 = consts
    cdim = hp + 2 * gn
    state_spec = pl.BlockSpec((seq_blk, hp, n_state), lambda b, c: (b, 0, 0))
    in_specs = [spec(hp, 0), spec(hp, 1), spec(2 * gn, bc_blk), spec(LANES, 0),
                pl.BlockSpec((seq_blk, SUBLANES, cdim), lambda b, c: (b, 0, 0)),
                _full(cw.shape), _full(cb.shape), _full(dtb.shape), _full(alog.shape), _full(dexp.shape),
                _full(anorm.shape), _full(e64.shape), _full(e64t.shape), _full(e128.shape), state_spec]
    scratch = [pltpu.VMEM(lead + (hp, n_state), F32),
               pltpu.VMEM(lead + (SUBLANES + max(t_rows, SUBLANES), cdim), F32)]
    if t_rows < SSD_CHUNK:
        scratch += [pltpu.VMEM(lead + (SSD_CHUNK, LANES), F32), pltpu.VMEM(lead + (SSD_CHUNK, hp), F32),
                    pltpu.VMEM(lead + (SSD_CHUNK, gn), F32), pltpu.VMEM(lead + (SSD_CHUNK, hp), F32)]
    kern = functools.partial(_ssd_kernel, n_sub=n_sub, t_rows=t_rows, n_heads=n_heads, head_dim=head_dim,
                             n_groups=n_groups, n_state=n_state, conv_w=conv_w)
    return pl.pallas_call(
        kern,
        grid=grid,
        in_specs=in_specs,
        out_specs=[y_spec, state_spec],
        out_shape=[jax.ShapeDtypeStruct(y_shape, out_dtype),
                   jax.ShapeDtypeStruct((n_seq, hp, n_state), F32)],
        scratch_shapes=scratch,
        compiler_params=_params("arbitrary", "arbitrary"),
    )(z_x_bc, z_x_bc, z_x_bc, dt_g, conv_init, cw, cb, dtb, alog, dexp, anorm, e64, e64t, e128, h0)


def _gla_kernel(*refs, n_sub, **kw):
    if n_sub is None:
        return _gla_seq(*refs, **kw)
    seq_major = (0, 1, 2, 3, 4, 8, 9, 10)
    n_fixed = 11
    for s in range(n_sub):
        view = [r.at[s] if (i in seq_major or i >= n_fixed) else r for i, r in enumerate(refs)]
        _gla_seq(*view, **kw)


def _gla_seq(q_ref, k_ref, v_ref, r_ref, g_ref, wa_ref, ba_ref, gnorm_ref, s0_ref, y_ref, sout_ref,
             s_ref, *pad_refs, t_rows, t_chunk, n_heads, dk, dv):
    c = pl.program_id(1)
    tp = GLA_CHUNK
    padded = t_chunk < tp
    if padded:
        la_pad, kt_pad, kw_pad, v_pad = pad_refs

    @pl.when(c == 0)
    def _init():
        s_ref[...] = s0_ref[...]
        for r in pad_refs:
            r[...] = jnp.zeros(r.shape, r.dtype)

    wa_b = wa_ref[...].astype(BF16)
    tril = (_iota((t_chunk, tp), 1) <= _iota((t_chunk, tp), 0))
    tril_b = tril.astype(BF16)
    ones_b = jnp.ones((tp, LANES), BF16)
    scale = dk ** -0.5
    for sub in range(t_rows // t_chunk):
        rows = slice(sub * t_chunk, (sub + 1) * t_chunk)
        la_raw = _dot(g_ref[rows, :].astype(BF16), wa_b) + ba_ref[...]
        la = -_softplus(-la_raw) * (1.0 / GLA_GATE_TAU)
        la_p = _pad_rows(la, la_pad if padded else None, t_chunk)
        bcum = _dot_f32_rhs(tril_b, la_p)
        b_last = bcum[t_chunk - 1:t_chunk, :]
        kk = k_ref[rows, :].astype(F32)
        q_t = (q_ref[rows, :].astype(F32) * scale * jnp.exp(bcum)).astype(BF16)
        k_t = _pad_rows(kk * jnp.exp(-bcum), kt_pad if padded else None, t_chunk).astype(BF16)
        k_w = _pad_rows(kk * jnp.exp(b_last - bcum), kw_pad if padded else None, t_chunk).astype(BF16)
        v_p = _pad_rows(v_ref[rows, :].astype(F32), v_pad if padded else None, t_chunk).astype(BF16)
        outs = []
        for h in range(n_heads):
            ks = slice(h * dk, (h + 1) * dk)
            vs = slice(h * dv, (h + 1) * dv)
            att = jnp.where(tril, _dot(q_t[:, ks], k_t[:, ks], NT_DIMS), 0.0)
            s_prev = s_ref[h]
            o = _dot(att.astype(BF16), v_p[:, vs]) + _dot(q_t[:, ks], s_prev.astype(BF16))
            ds = _dot(k_w[:, ks], v_p[:, vs], TN_DIMS)
            dcol = _dot_f32_lhs(la_p[:, ks], ones_b, TN_DIMS)
            dec = jnp.exp(dcol)
            s_ref[h] = s_prev * jnp.concatenate([dec] * (dv // LANES), axis=1) + ds
            ms = jnp.mean(o * o, axis=-1, keepdims=True)
            outs.append((o * lax.rsqrt(ms + EPS)) * gnorm_ref[:, vs])
        y = jnp.concatenate(outs, axis=1) * _silu(r_ref[rows, :].astype(F32))
        y_ref[rows, :] = y.astype(y_ref.dtype)

    @pl.when(c == pl.num_programs(1) - 1)
    def _fin():
        sout_ref[...] = s_ref[...]


def _gla(qkvr, dt_g, s0, consts, *, n_seq, n_steps, t_rows, t_chunk, dims, n_sub, out_dtype):
    n_heads, dk, dv = dims
    kd, vd = n_heads * dk, n_heads * dv
    assert dk % LANES == 0 and dv % LANES == 0 and vd % kd == 0 and t_rows % t_chunk == 0
    lead = () if n_sub is None else (n_sub,)
    if n_sub is not None:
        assert n_steps == 1 and n_seq % n_sub == 0

        def spec(width, blk):
            return pl.BlockSpec((n_sub, t_rows, width), lambda b, c: (b, 0, blk))
        y_shape = (n_seq, t_rows, vd)
        y_spec = pl.BlockSpec((n_sub, t_rows, vd), lambda b, c: (b, 0, 0))
        seq_blk = n_sub
        grid = (n_seq // n_sub, 1)
    else:
        def spec(width, blk):
            return pl.BlockSpec((t_rows, width), lambda b, c: (b * n_steps + c, blk))
        y_shape = (n_seq * n_steps * t_rows, vd)
        y_spec = pl.BlockSpec((t_rows, vd), lambda b, c: (b * n_steps + c, 0))
        seq_blk = None
        grid = (n_seq, n_steps)
    wa, ba, gnorm = consts
    state_spec = pl.BlockSpec((seq_blk, n_heads, dk, dv), lambda b, c: (b, 0, 0, 0))
    in_specs = [spec(kd, 0), spec(kd, 1), spec(vd, (2 * kd) // vd), spec(vd, (2 * kd) // vd + 1),
                spec(LANES, 1), _full(wa.shape), _full(ba.shape), _full(gnorm.shape), state_spec]
    scratch = [pltpu.VMEM(lead + (n_heads, dk, dv), F32)]
    if t_chunk < GLA_CHUNK:
        scratch += [pltpu.VMEM(lead + (GLA_CHUNK, kd), F32), pltpu.VMEM(lead + (GLA_CHUNK, kd), F32),
                    pltpu.VMEM(lead + (GLA_CHUNK, kd), F32), pltpu.VMEM(lead + (GLA_CHUNK, vd), F32)]
    kern = functools.partial(_gla_kernel, n_sub=n_sub, t_rows=t_rows, t_chunk=t_chunk, n_heads=n_heads,
                             dk=dk, dv=dv)
    return pl.pallas_call(
        kern,
        grid=grid,
        in_specs=in_specs,
        out_specs=[y_spec, state_spec],
        out_shape=[jax.ShapeDtypeStruct(y_shape, out_dtype),
                   jax.ShapeDtypeStruct((n_seq, n_heads, dk, dv), F32)],
        scratch_shapes=scratch,
        compiler_params=_params("arbitrary", "arbitrary"),
    )(qkvr, qkvr, qkvr, qkvr, dt_g, wa, ba, gnorm, s0)


def _cmix_kernel(u_ref, v_ref, w_ref, bsx_ref, lng_ref, lnb_ref, y_ref, vn_ref, *, n_groups, gd, n_ptiles):
    i = pl.program_id(0)
    for g in range(n_groups):
        cols = slice(g * gd, (g + 1) * gd)
        v = v_ref[:, cols].astype(F32)
        mu = jnp.mean(v, axis=-1, keepdims=True)
        d = v - mu
        var = jnp.mean(d * d, axis=-1, keepdims=True)
        vn = (d * lax.rsqrt(var + EPS)) * lng_ref[:, cols] + lnb_ref[:, cols]
        mixed = _dot(w_ref[g].astype(BF16), vn.astype(BF16)) + bsx_ref[g]
        y_ref[:, cols] = (u_ref[:, cols].astype(F32) * mixed).astype(y_ref.dtype)

        @pl.when(i >= n_ptiles)
        def _keep():
            vn_ref[:, cols] = vn


def _cmix(uv, w_sel, bsx_sel, ln_g, ln_b, *, rows_p, chunk, n_groups, gd):
    n_rows = uv.shape[0]
    cw = n_groups * gd
    n_ptiles = rows_p // chunk
    kern = functools.partial(_cmix_kernel, n_groups=n_groups, gd=gd, n_ptiles=n_ptiles)
    sel = lambda i: (jnp.where(i >= n_ptiles, 1, 0), 0, 0, 0)
    return pl.pallas_call(
        kern,
        grid=(n_rows // chunk,),
        in_specs=[pl.BlockSpec((chunk, cw), lambda i: (i, 0)), pl.BlockSpec((chunk, cw), lambda i: (i, 1)),
                  pl.BlockSpec((None, n_groups, chunk, chunk), sel),
                  pl.BlockSpec((None, n_groups, chunk, gd), sel),
                  _full((1, cw)), _full((1, cw))],
        out_specs=[pl.BlockSpec((chunk, cw), lambda i: (i, 0)),
                   pl.BlockSpec((chunk, cw), lambda i: (jnp.maximum(i - n_ptiles, 0), 0))],
        out_shape=[jax.ShapeDtypeStruct((n_rows, cw), BF16),
                   jax.ShapeDtypeStruct((n_rows - rows_p, cw), F32)],
        compiler_params=_params("arbitrary"),
    )(uv, uv, w_sel, bsx_sel, ln_g.reshape(1, cw), ln_b.reshape(1, cw))


def _weights_changed(te_ref, t):
    return (t == 0) | (te_ref[t] != te_ref[jnp.maximum(t - 1, 0)])


def _ffn_up_kernel(te_ref, nu_ref, x_ref, w1_ref, w3_ref, o_ref, w1b_ref, w3b_ref):
    t = pl.program_id(1)

    @pl.when(_weights_changed(te_ref, t))
    def _cast():
        w1b_ref[...] = w1_ref[...].astype(BF16)
        w3b_ref[...] = w3_ref[...].astype(BF16)

    @pl.when(t < nu_ref[0])
    def _compute():
        x = x_ref[...]
        tf = o_ref.shape[1]
        cw = MXU_COLS if tf % MXU_COLS == 0 else tf
        for c0 in range(0, tf, cw):
            a = _dot(x, w1b_ref[:, c0:c0 + cw])
            b = _dot(x, w3b_ref[:, c0:c0 + cw])
            o_ref[:, c0:c0 + cw] = (_silu(a) * b).astype(o_ref.dtype)

    @pl.when(t >= nu_ref[0])
    def _unused():
        o_ref[...] = jnp.zeros(o_ref.shape, o_ref.dtype)


def _ffn_up(x, w1, w3, tile_expert, n_used, *, tm, tf):
    n_rows, d = x.shape
    f = w1.shape[2]
    grid = (pl.cdiv(f, tf), n_rows // tm)
    return pl.pallas_call(
        _ffn_up_kernel,
        grid_spec=pltpu.PrefetchScalarGridSpec(
            num_scalar_prefetch=2,
            grid=grid,
            in_specs=[pl.BlockSpec((tm, d), lambda j, t, te, nu: (t, 0)),
                      pl.BlockSpec((None, d, tf), lambda j, t, te, nu: (te[t], 0, j)),
                      pl.BlockSpec((None, d, tf), lambda j, t, te, nu: (te[t], 0, j))],
            out_specs=pl.BlockSpec((tm, tf), lambda j, t, te, nu: (t, j)),
            scratch_shapes=[pltpu.VMEM((d, tf), BF16), pltpu.VMEM((d, tf), BF16)]),
        out_shape=jax.ShapeDtypeStruct((n_rows, f), BF16),
        compiler_params=_params("arbitrary", "arbitrary"),
    )(tile_expert, n_used, x, w1, w3)


def _ffn_down_kernel(te_ref, nu_ref, g_ref, w2_ref, o_ref, w2b_ref):
    t = pl.program_id(1)

    @pl.when(_weights_changed(te_ref, t))
    def _cast():
        w2b_ref[...] = w2_ref[...].astype(BF16)

    @pl.when(t < nu_ref[0])
    def _compute():
        o_ref[...] = _dot(g_ref[...], w2b_ref[...])

    @pl.when(t >= nu_ref[0])
    def _unused():
        o_ref[...] = jnp.zeros(o_ref.shape, o_ref.dtype)


def _ffn_down(g, w2, tile_expert, n_used, *, tm):
    n_rows, f = g.shape
    d = w2.shape[2]
    tn = _fit_tile(d, lambda t: f * t * (2 * 4 + 2) + 2 * tm * f * 2 + 2 * tm * t * 4)
    return pl.pallas_call(
        _ffn_down_kernel,
        grid_spec=pltpu.PrefetchScalarGridSpec(
            num_scalar_prefetch=2,
            grid=(d // tn, n_rows // tm),
            in_specs=[pl.BlockSpec((tm, f), lambda j, t, te, nu: (t, 0)),
                      pl.BlockSpec((None, f, tn), lambda j, t, te, nu: (te[t], 0, j))],
            out_specs=pl.BlockSpec((tm, tn), lambda j, t, te, nu: (t, j)),
            scratch_shapes=[pltpu.VMEM((f, tn), BF16)]),
        out_shape=jax.ShapeDtypeStruct((n_rows, d), F32),
        compiler_params=_params("arbitrary", "arbitrary"),
    )(tile_expert, n_used, g, w2)


def _router_kernel(h_ref, w_ref, idx_ref, gate_ref, *, n_exp):
    logits = jnp.dot(h_ref[...], w_ref[...], preferred_element_type=F32, precision=lax.Precision.HIGHEST)
    lane = _iota(logits.shape, 1)
    neg = jnp.float32(-jnp.inf)
    l1 = jnp.where(lane < n_exp, logits, neg)
    m1 = jnp.max(l1, axis=-1, keepdims=True)
    i1 = jnp.min(jnp.where(l1 == m1, lane, LANES), axis=-1, keepdims=True)
    l2 = jnp.where(lane == i1, neg, l1)
    m2 = jnp.max(l2, axis=-1, keepdims=True)
    i2 = jnp.min(jnp.where(l2 == m2, lane, LANES), axis=-1, keepdims=True)
    e = jnp.exp(m2 - m1)
    g1 = 1.0 / (1.0 + e)
    g2 = e / (1.0 + e)
    idx_ref[...] = jnp.where(lane == 0, i1, jnp.where(lane == 1, i2, 0))
    gate_ref[...] = jnp.where(lane == 0, g1, jnp.where(lane == 1, g2, 0.0))


def _router(h, w_pad, *, n_exp, tm):
    n_rows, d = h.shape
    kern = functools.partial(_router_kernel, n_exp=n_exp)
    return pl.pallas_call(
        kern,
        grid=(n_rows // tm,),
        in_specs=[pl.BlockSpec((tm, d), lambda i: (i, 0)), _full(w_pad.shape)],
        out_specs=[pl.BlockSpec((tm, LANES), lambda i: (i, 0)), pl.BlockSpec((tm, LANES), lambda i: (i, 0))],
        out_shape=[jax.ShapeDtypeStruct((n_rows, LANES), I32), jax.ShapeDtypeStruct((n_rows, LANES), F32)],
        compiler_params=_params("arbitrary"),
    )(h, w_pad)


def _token_copy(src_hbm, tok, dst_ref, slot, sem):
    return pltpu.make_async_copy(src_hbm.at[tok], dst_ref.at[slot], sem)


DMA_UNROLL = 8


def _gather_kernel(idx_ref, src_hbm, o_ref, sem, *, tm):
    def _start(r0, carry):
        for j in range(DMA_UNROLL):
            r = r0 * DMA_UNROLL + j
            _token_copy(src_hbm, idx_ref[0, 0, r], o_ref, r, sem).start(priority=j % 2)
        return carry

    def _wait(r, carry):
        _token_copy(src_hbm, idx_ref[0, 0, r], o_ref, r, sem).wait()
        return carry

    lax.fori_loop(0, tm // DMA_UNROLL, _start, 0)
    lax.fori_loop(0, tm, _wait, 0, unroll=DMA_UNROLL)


def _gather_tokens(src, row_idx, *, tm):
    n_out = row_idx.shape[0]
    n_tiles = n_out // tm
    slab = src.shape[1:]
    kern = functools.partial(_gather_kernel, tm=tm)
    return pl.pallas_call(
        kern,
        grid=(n_tiles,),
        in_specs=[pl.BlockSpec((1, 1, tm), lambda i: (i, 0, 0), memory_space=pltpu.SMEM),
                  pl.BlockSpec(memory_space=pl.ANY)],
        out_specs=pl.BlockSpec((tm,) + slab, lambda i: (i, 0, 0)),
        out_shape=jax.ShapeDtypeStruct((n_out,) + slab, src.dtype),
        scratch_shapes=[pltpu.SemaphoreType.DMA(())],
        compiler_params=_params("arbitrary"),
    )(row_idx.reshape(n_tiles, 1, tm), src)


def _row_copy(src_hbm, row, dst_ref, slot, sem):
    return pltpu.make_async_copy(src_hbm.at[pl.ds(row, 1), :], dst_ref.at[pl.ds(slot, 1), :], sem)


def _combine_kernel(pos_ref, ys_hbm, g0_ref, g1_ref, x_ref, gp_ref, gs_ref, gain_ref, op_ref, os_ref,
                    b0_ref, b1_ref, sem, *, tm, n_ptiles):
    i = pl.program_id(0)

    def _start(r0, carry):
        for j in range(DMA_UNROLL // 2):
            r = r0 * (DMA_UNROLL // 2) + j
            _row_copy(ys_hbm, pos_ref[0, 0, r], b0_ref, r, sem).start(priority=0)
            _row_copy(ys_hbm, pos_ref[0, 0, tm + r], b1_ref, r, sem).start(priority=1)
        return carry

    def _wait(r, carry):
        _row_copy(ys_hbm, pos_ref[0, 0, r], b0_ref, r, sem).wait()
        _row_copy(ys_hbm, pos_ref[0, 0, tm + r], b1_ref, r, sem).wait()
        return carry

    lax.fori_loop(0, tm // (DMA_UNROLL // 2), _start, 0)
    lax.fori_loop(0, tm, _wait, 0, unroll=DMA_UNROLL // 2)
    def finish(gate_ref, o_ref):
        y = g0_ref[...] * b0_ref[...] + g1_ref[...] * b1_ref[...]
        x = x_ref[...] + gate_ref[...] * y
        ms = jnp.mean(x * x, axis=-1, keepdims=True)
        o_ref[...] = (x * lax.rsqrt(ms + EPS)) * gain_ref[...]

    @pl.when(i < n_ptiles)
    def _prompt():
        finish(gp_ref, op_ref)

    @pl.when(i >= n_ptiles)
    def _sample():
        finish(gs_ref, os_ref)


def _combine(ys, pos, g0, g1, x, plan, mods, gate_col, gain):
    d = x.shape[1]
    tm = plan.tm
    n_tiles = plan.n_tiles
    mod_p, mod_s = mods
    pos_t = jnp.concatenate([pos[:, 0].reshape(n_tiles, 1, tm), pos[:, 1].reshape(n_tiles, 1, tm)], axis=2)
    col_spec = pl.BlockSpec((tm, 1), lambda i: (i, 0))
    kern = functools.partial(_combine_kernel, tm=tm, n_ptiles=plan.n_ptiles)
    return pl.pallas_call(
        kern,
        grid=(n_tiles,),
        in_specs=[pl.BlockSpec((1, 1, 2 * tm), lambda i: (i, 0, 0), memory_space=pltpu.SMEM),
                  pl.BlockSpec(memory_space=pl.ANY), col_spec, col_spec, plan.row(d),
                  plan.prompt_mod(d, gate_col), plan.sample_row(d, gate_col), _full((1, d))],
        out_specs=[plan.prompt_row(d), plan.sample_row(d)],
        out_shape=[jax.ShapeDtypeStruct((plan.rows_p, d), F32), jax.ShapeDtypeStruct((plan.rows_s, d), F32)],
        scratch_shapes=[pltpu.VMEM((tm, d), F32), pltpu.VMEM((tm, d), F32), pltpu.SemaphoreType.DMA(())],
        compiler_params=_params("arbitrary"),
    )(pos_t, ys, g0, g1, x, mod_p, mod_s, gain.reshape(1, d))


def _routing_tables(top_idx, n_exp, tm, n_tiles):
    n_tok = top_idx.shape[0]
    n_slots = TOP_K * n_tok
    flat_e = top_idx.reshape(-1)
    order = jnp.argsort(flat_e, stable=True).astype(I32)
    inv_order = jnp.argsort(order).astype(I32)
    counts = jnp.sum((flat_e[:, None] == jnp.arange(n_exp, dtype=I32)[None, :]).astype(I32), axis=0)
    padded = ((counts + tm - 1) // tm) * tm
    start = jnp.cumsum(counts) - counts
    end_p = jnp.cumsum(padded)
    start_p = end_p - padded
    shift = start_p - start
    pos = (inv_order + shift[flat_e]).reshape(n_tok, TOP_K)
    tile = jnp.arange(n_tiles, dtype=I32)
    tile_expert = jnp.minimum(jnp.sum(((end_p // tm)[None, :] <= tile[:, None]).astype(I32), axis=1), n_exp - 1)
    row = jnp.arange(n_tiles * tm, dtype=I32)
    row_e = jnp.repeat(tile_expert, tm)
    sorted_idx = row - shift[row_e]
    valid = (row - start_p[row_e]) < counts[row_e]
    row_src = jnp.where(valid, order[jnp.clip(sorted_idx, 0, n_slots - 1)] // TOP_K, 0).astype(I32)
    n_used = (end_p[n_exp - 1:] // tm).astype(I32)
    return row_src, pos, tile_expert, n_used


def kernel(x_prompt, x_sample, state_ssm, state_conv, state_gla, c_prompt, c_sample, ada_w0, ada_b0, norm_mix0, norm_ffn0, w_in0, conv_w, conv_b, dt_bias, a_log, d_skip, a_norm, gla_wa2, gla_ba, gla_norm, w_out0, ffn_w1, ffn_w3, ffn_w2, ada_w1, ada_b1, norm_mix1, norm_ffn1, c_w_in, c_ln_g, c_ln_b, c_ws, c_bs, c_w_out, router_w, moe_w1, moe_w3, moe_w2, norm_f):
    bp, lp, d = x_prompt.shape
    bs, ls, _ = x_sample.shape
    n_even = state_ssm.shape[0]
    assert n_even == 1 and ada_w1.shape[0] == 1
    _, _, n_heads, head_dim, n_state = state_ssm.shape
    conv_taps, conv_dim = conv_w.shape[1], conv_w.shape[2]
    hp = n_heads * head_dim
    gn = (conv_dim - hp) // 2
    n_groups = gn // n_state
    _, _, g_heads, dk, dv = state_gla.shape
    kd, vd = g_heads * dk, g_heads * dv
    rank = gla_wa2.shape[1]
    c_groups, c_chunk = c_ws.shape[1], c_ws.shape[2]
    c_width = c_w_out.shape[1]
    gd = c_width // c_groups
    n_exp, d_ff = moe_w1.shape[1], moe_w1.shape[3]
    rows_p, rows_s = bp * lp, bs * ls
    n_rows = rows_p + rows_s
    assert lp % SSD_CHUNK == 0 and lp % c_chunk == 0 and c_chunk % ls == 0 and ls <= GLA_CHUNK
    assert ls >= conv_taps - 1 and conv_taps - 1 <= SUBLANES and rank <= LANES and d % LANES == 0

    plan = _Rows(rows_p, rows_s, lp, _pick_tile(math.gcd(lp, rows_s), (256, 128, 64, 32, 16, 8)))
    tm_mm = _pick_tile(n_rows, (1088, 1024, 512, 256, 128, 64, 32, 16, 8))
    tm_dual = rows_s
    assert rows_p % tm_dual == 0
    tm_ffn = _pick_tile(n_rows, (512, 256, 128, 64, 32, 16, 8))

    c_all = jnp.concatenate([c_prompt, c_sample], axis=0)

    def ada(w, b):
        mod = _matmul([c_all], w, tm=c_all.shape[0], bias=b, lhs_act="silu")
        return mod[:bp].reshape(bp, 1, 6 * d), jnp.repeat(mod[bp:], ls, axis=0)

    x0 = (x_prompt.reshape(rows_p, d), x_sample.reshape(rows_s, d))

    mods0 = ada(ada_w0[0], ada_b0[0])
    h = _norm(x0, norm_mix0[0], plan, mods0, mod_cols=(0, 1))

    w_in = w_in0[0]
    off_dt = hp + conv_dim
    off_q = off_dt + n_heads
    off_g = off_q + 2 * kd + 2 * vd
    assert off_dt % LANES == 0
    zxbc = _matmul([h], w_in, tm=tm_mm, n_cols=off_dt, out_dtype=BF16)
    qkvr = _matmul([h], w_in, tm=tm_mm, col0=off_q, n_cols=off_g - off_q, out_dtype=BF16)
    w_small = jnp.concatenate([jnp.pad(w_in[:, off_dt:off_q], ((0, 0), (0, LANES - n_heads))),
                               jnp.pad(w_in[:, off_g:], ((0, 0), (0, LANES - rank)))], axis=1)
    dt_g = _matmul([h], w_small, tm=tm_mm)

    head_of_col = jnp.arange(hp, dtype=I32) // head_dim
    e64 = (jnp.arange(LANES, dtype=I32)[:, None] == head_of_col[None, :]).astype(BF16)
    e128 = (jnp.arange(LANES, dtype=I32)[:, None]
            == (jnp.arange(LANES * n_heads, dtype=I32) // LANES)[None, :]).astype(BF16)
    ssd_consts = (conv_w[0], conv_b[0].reshape(1, conv_dim),
                  jnp.pad(dt_bias[0], (0, LANES - n_heads)).reshape(1, LANES),
                  jnp.pad(a_log[0], (0, LANES - n_heads)).reshape(1, LANES),
                  jnp.repeat(d_skip[0], head_dim).reshape(1, hp), a_norm[0].reshape(1, hp), e64, e64.T, e128)
    ssd_dims = (n_heads, head_dim, n_groups, n_state, conv_taps)
    ya_p, ssm_p = _ssd(zxbc, dt_g, jnp.zeros((bp, SUBLANES, conv_dim), F32), jnp.zeros((bp, hp, n_state), F32),
                       ssd_consts, n_seq=bp, n_chunks=lp // SSD_CHUNK, t_rows=SSD_CHUNK, dims=ssd_dims,
                       n_sub=None, out_dtype=BF16)
    zxbc_s = zxbc[rows_p:].reshape(bs, ls, off_dt).astype(F32)
    dt_g_s = dt_g[rows_p:].reshape(bs, ls, 2 * LANES)
    conv_init_s = jnp.pad(state_conv[0], ((0, 0), (SUBLANES - (conv_taps - 1), 0), (0, 0)))
    ya_s, ssm_s = _ssd(zxbc_s, dt_g_s, conv_init_s, state_ssm[0].reshape(bs, hp, n_state), ssd_consts,
                       n_seq=bs, n_chunks=1, t_rows=ls, dims=ssd_dims, n_sub=1, out_dtype=F32)

    gla_consts = (jnp.pad(gla_wa2[0], ((0, LANES - rank), (0, 0))), gla_ba[0].reshape(1, kd),
                  gla_norm[0].reshape(1, vd))
    gla_dims = (g_heads, dk, dv)
    t_gla = 2 * GLA_CHUNK
    yb_p, gla_p = _gla(qkvr, dt_g, jnp.zeros((bp, g_heads, dk, dv), F32), gla_consts, n_seq=bp,
                       n_steps=lp // t_gla, t_rows=t_gla, t_chunk=GLA_CHUNK, dims=gla_dims, n_sub=None,
                       out_dtype=BF16)
    qkvr_s = qkvr[rows_p:].reshape(bs, ls, off_g - off_q).astype(F32)
    yb_s, gla_s = _gla(qkvr_s, dt_g_s, state_gla[0], gla_consts, n_seq=bs, n_steps=1, t_rows=ls, t_chunk=ls,
                       dims=gla_dims, n_sub=2 if bs % 2 == 0 else 1, out_dtype=F32)

    ya_s2 = ya_s.reshape(rows_s, hp).astype(BF16)
    yb_s2 = yb_s.reshape(rows_s, vd).astype(BF16)
    y_mix = _matmul([(ya_p, ya_s2), (yb_p, yb_s2)], w_out0[0], tm=tm_dual)
    x1, h = _norm(x0, norm_ffn0[0], plan, mods0, resid=(y_mix, mods0, 2), mod_cols=(3, 4))

    tf = 512 if d_ff > 512 else d_ff
    tm_down = _pick_tile(n_rows, (544, 512, 256, 128, 64, 32, 16, 8))

    def one_expert(tm):
        return jnp.zeros((n_rows // tm,), I32), jnp.full((1,), n_rows // tm, I32)

    g_act = _ffn_up(h, ffn_w1, ffn_w3, *one_expert(tm_mm), tm=tm_mm, tf=tf)
    y_ffn = _ffn_down(g_act, ffn_w2, *one_expert(tm_down), tm=tm_down)

    mods1 = ada(ada_w1[0], ada_b1[0])
    x2, h = _norm(x1, norm_mix1[0], plan, mods1, resid=(y_ffn, mods0, 5), mod_cols=(0, 1))
    uv = _matmul([h], c_w_in[0], tm=tm_mm, act="gelu", out_dtype=BF16)
    tril_ws = jnp.tril(c_ws[0])
    w_short = jnp.tril(c_ws[0][:, :ls, :ls])
    eye = jnp.eye(c_chunk // ls, dtype=F32)
    w_kron = jnp.einsum("ab,gts->gatbs", eye, w_short).reshape(c_groups, c_chunk, c_chunk)
    w_sel = jnp.stack([tril_ws, w_kron])
    bs_p = jnp.broadcast_to(c_bs[0][:, :, None], (c_groups, c_chunk, gd))
    bs_s = jnp.broadcast_to(jnp.tile(c_bs[0][:, :ls], (1, c_chunk // ls))[:, :, None], (c_groups, c_chunk, gd))
    y_c, vn_s = _cmix(uv, w_sel, jnp.stack([bs_p, bs_s]), c_ln_g[0], c_ln_b[0], rows_p=rows_p, chunk=c_chunk,
                      n_groups=c_groups, gd=gd)
    y_mix1 = _matmul([y_c], c_w_out[0], tm=tm_mm)
    x3, h32 = _norm(x2, norm_ffn1[0], plan, mods1, resid=(y_mix1, mods1, 2), mod_cols=(3, 4), h_dtype=F32)

    idx_l, gate_l = _router(h32, jnp.pad(router_w[0], ((0, 0), (0, LANES - n_exp))), n_exp=n_exp, tm=tm_ffn)
    top_idx, gates = idx_l[:, :TOP_K], gate_l[:, :TOP_K]
    n_moe_tiles = (TOP_K * n_rows + n_exp * (tm_ffn - 1)) // tm_ffn
    row_src, pos, tile_expert, n_used = _routing_tables(top_idx, n_exp, tm_ffn, n_moe_tiles)
    assert (d // LANES) in (1, 2, 4, 8) or d % (BF16_ROWS * LANES) == 0
    h_tok = h32.astype(BF16).reshape(n_rows, d // LANES, LANES)
    xs = _gather_tokens(h_tok, row_src, tm=tm_ffn).reshape(n_moe_tiles * tm_ffn, d)
    g_moe = _ffn_up(xs, moe_w1[0], moe_w3[0], tile_expert, n_used, tm=tm_ffn, tf=tf)
    ys = _ffn_down(g_moe, moe_w2[0], tile_expert, n_used, tm=tm_ffn)
    y_p, y_s = _combine(ys, pos, gates[:, 0:1], gates[:, 1:2], x3, plan, mods1, 5, norm_f)

    y_prompt = y_p.reshape(bp, lp, d)
    y_sample = y_s.reshape(bs, ls, d)
    n_hist = conv_taps - 1
    conv_prompt = jnp.stack([lax.slice(zxbc, ((b + 1) * lp - n_hist, hp), ((b + 1) * lp, off_dt))
                             for b in range(bp)]).astype(F32)[None]
    conv_full_s = jnp.concatenate([state_conv[0], zxbc_s[:, :, hp:]], axis=1)
    conv_sample = conv_full_s[:, conv_full_s.shape[1] - n_hist:][None]
    ssm_prompt = ssm_p.reshape(1, bp, n_heads, head_dim, n_state)
    ssm_sample = ssm_s.reshape(1, bs, n_heads, head_dim, n_state)
    gla_prompt = gla_p[None]
    gla_sample = gla_s[None]
    cmlp_v_sample = vn_s.reshape(1, bs, ls, c_groups, gd)
    return (y_prompt, y_sample, ssm_prompt, conv_prompt, gla_prompt, ssm_sample, conv_sample, gla_sample,
            cmlp_v_sample)
```
